```python
import jax, jax.numpy as jnp
from jax import lax
import numpy as np

D_MODEL = 4096
BATCH = 4
SEQ = 2048
DEPTH = 1
DEC_BATCH = 128
DEC_SEQ = 4
PAST_LEN = 16384
PAGE_SIZE = 128

MLA_HEADS = 16
MLA_NOPE_DIM = 128
MLA_ROPE_DIM = 64
MLA_V_DIM = 128
MLA_Q_LORA = 1024
MLA_KV_LORA = 512
MLA_LAT = MLA_KV_LORA + MLA_ROPE_DIM
MLA_SCALE = (MLA_NOPE_DIM + MLA_ROPE_DIM) ** -0.5
ROPE_THETA = 10000.0
Q_BLOCK = 128
MOBA_HEADS = 16
MOBA_KV_HEADS = 2
MOBA_GROUP = MOBA_HEADS // MOBA_KV_HEADS
MOBA_HEAD_DIM = 128
MOBA_BLOCK = 256
MOBA_TOPK = 3
MOBA_N_SEL = MOBA_TOPK + 1
MOBA_SCALE = MOBA_HEAD_DIM ** -0.5
MOBA_Q_CHUNK = 8
IN_SPLITS = (MLA_Q_LORA, MLA_LAT, MOBA_HEADS * MOBA_HEAD_DIM, MOBA_KV_HEADS * MOBA_HEAD_DIM,
             MOBA_KV_HEADS * MOBA_HEAD_DIM, D_MODEL, D_MODEL)
IN_COLS = sum(IN_SPLITS)
BRANCH_A_WIDTH = MLA_HEADS * MLA_V_DIM
BRANCH_B_WIDTH = MOBA_HEADS * MOBA_HEAD_DIM
MOE_GROUPS = 8
MOE_EXPERTS_PER_GROUP = 8
MOE_N_EXPERTS = MOE_GROUPS * MOE_EXPERTS_PER_GROUP
MOE_TOPK = 2
MOE_HIDDEN = 512
MOE_ROW_BLOCK = 128
NORM_EPS = 1e-6
NEG = -1e30

kernel_name = "mla_moba_gated_hier_moe_step"


def rms_norm(x, g):
    xf = x.astype(jnp.float32)
    y = xf * lax.rsqrt(jnp.mean(xf * xf, axis=-1, keepdims=True) + NORM_EPS)
    return (y * g.astype(jnp.float32)).astype(x.dtype)


def rope(x, pos):
    half = MLA_ROPE_DIM // 2
    inv = ROPE_THETA ** (-jnp.arange(half, dtype=jnp.float32) / half)
    ang = pos.astype(jnp.float32)[:, None] * inv[None, :]
    cos, sin = jnp.cos(ang)[:, None, :], jnp.sin(ang)[:, None, :]
    xf = x.astype(jnp.float32)
    x1, x2 = xf[..., :half], xf[..., half:]
    return jnp.concatenate([x1 * cos - x2 * sin, x1 * sin + x2 * cos], axis=-1).astype(x.dtype)


def alibi_slopes():
    return 2.0 ** (-8.0 * jnp.arange(1, MOBA_HEADS + 1, dtype=jnp.float32) / MOBA_HEADS)


def project_in(h, pos, w_in, g_q_norm, g_kv_norm, w_uq, w_uk):
    n, t = h.shape[0], h.shape[1]
    z = jnp.einsum('ntd,dc->ntc', h, w_in)
    cq, ckr, qb, kb, vb, ga, gb = jnp.split(z, np.cumsum(IN_SPLITS)[:-1].tolist(), axis=-1)
    cq = rms_norm(cq, g_q_norm)
    q = jnp.einsum('ntc,chd->nthd', cq, w_uq)
    q_rope = rope(q[..., MLA_NOPE_DIM:], pos)
    q_lat = jnp.einsum('nthd,chd->nthc', q[..., :MLA_NOPE_DIM], w_uk)
    c_kv = rms_norm(ckr[..., :MLA_KV_LORA], g_kv_norm)
    k_rope = rope(ckr[..., MLA_KV_LORA:][:, :, None, :], pos)[:, :, 0, :]
    mla_rows = jnp.concatenate([c_kv, k_rope], axis=-1)
    moba_q = qb.reshape(n, t, MOBA_HEADS, MOBA_HEAD_DIM)
    kv_rows = jnp.stack([kb.reshape(n, t, MOBA_KV_HEADS, MOBA_HEAD_DIM),
                         vb.reshape(n, t, MOBA_KV_HEADS, MOBA_HEAD_DIM)], axis=2)
    return q_lat, q_rope, mla_rows, moba_q, kv_rows, ga, gb


def mla_core(q_lat, q_rope, c_kv, k_rope, q_pos, k_pos):
    s = (jnp.einsum('qhc,sc->hqs', q_lat, c_kv) + jnp.einsum('qhr,sr->hqs', q_rope, k_rope)).astype(jnp.float32) * MLA_SCALE
    s = jnp.where(k_pos[None, None, :] <= q_pos[None, :, None], s, NEG)
    p = jax.nn.softmax(s, axis=-1).astype(c_kv.dtype)
    return jnp.einsum('hqs,sc->qhc', p, c_kv)


def moba_blocks(k_all, v_all):
    L = k_all.shape[0]
    nb = -(-L // MOBA_BLOCK)
    pad = nb * MOBA_BLOCK - L
    kb = jnp.pad(k_all, ((0, pad), (0, 0), (0, 0))).reshape(nb, MOBA_BLOCK, MOBA_KV_HEADS, MOBA_HEAD_DIM).transpose(2, 0, 1, 3)
    vb = jnp.pad(v_all, ((0, pad), (0, 0), (0, 0))).reshape(nb, MOBA_BLOCK, MOBA_KV_HEADS, MOBA_HEAD_DIM).transpose(2, 0, 1, 3)
    k_mean = jnp.mean(kb.astype(jnp.float32), axis=2).astype(kb.dtype)
    return kb, vb, k_mean


def moba_core(q, q_pos, kb, vb, k_mean):
    nq, nb = q.shape[0], kb.shape[1]
    qg = q.reshape(nq, MOBA_KV_HEADS, MOBA_GROUP, MOBA_HEAD_DIM)
    gate = jnp.einsum('qkgd,knd->qkgn', qg, k_mean).reshape(nq, MOBA_HEADS, nb).astype(jnp.float32)
    own = q_pos // MOBA_BLOCK
    gate = jnp.where(jnp.arange(nb)[None, None, :] < own[:, None, None], gate, NEG)
    if nb < MOBA_TOPK:
        gate = jnp.concatenate([gate, jnp.full((nq, MOBA_HEADS, MOBA_TOPK - nb), NEG, jnp.float32)], axis=-1)
    top_val, top_idx = lax.top_k(gate, MOBA_TOPK)
    blk = jnp.concatenate([jnp.minimum(top_idx, nb - 1),
                           jnp.broadcast_to(own[:, None, None], (nq, MOBA_HEADS, 1))], axis=-1)
    blk_ok = jnp.concatenate([top_val > 0.5 * NEG, jnp.ones((nq, MOBA_HEADS, 1), dtype=bool)], axis=-1)
    kvh = (jnp.arange(MOBA_HEADS) // MOBA_GROUP)[None, :, None]
    k_sel = kb[kvh, blk]
    v_sel = vb[kvh, blk]
    k_pos = blk[..., None] * MOBA_BLOCK + jnp.arange(MOBA_BLOCK)
    dist = jnp.abs(q_pos[:, None, None, None] - k_pos).astype(jnp.float32)
    s = jnp.einsum('qhd,qhnkd->qhnk', q, k_sel).astype(jnp.float32) * MOBA_SCALE \
        - alibi_slopes()[None, :, None, None] * dist
    ok = blk_ok[..., None] & (k_pos <= q_pos[:, None, None, None])
    s = jnp.where(ok, s, NEG)
    p = jax.nn.softmax(s.reshape(nq, MOBA_HEADS, MOBA_N_SEL * MOBA_BLOCK), axis=-1).reshape(s.shape)
    return jnp.einsum('qhnk,qhnkd->qhd', p.astype(v_sel.dtype), v_sel)


def mla_prompt(q_lat, q_rope, mla_rows):
    n, t = q_lat.shape[0], q_lat.shape[1]
    pos = jnp.arange(t)
    nqb = t // Q_BLOCK
    c_kv, k_rope = mla_rows[..., :MLA_KV_LORA], mla_rows[..., MLA_KV_LORA:]
    ql = q_lat.reshape(n, nqb, Q_BLOCK, MLA_HEADS, MLA_KV_LORA).swapaxes(0, 1)
    qr = q_rope.reshape(n, nqb, Q_BLOCK, MLA_HEADS, MLA_ROPE_DIM).swapaxes(0, 1)
    qp = pos.reshape(nqb, Q_BLOCK)
    core = jax.vmap(mla_core, in_axes=(0, 0, 0, 0, None, None))
    out = lax.map(lambda a: core(a[0], a[1], c_kv, k_rope, a[2], pos), (ql, qr, qp))
    return out.swapaxes(0, 1).reshape(n, t, MLA_HEADS, MLA_KV_LORA)


def moba_prompt(moba_q, kv_rows):
    n, t = moba_q.shape[0], moba_q.shape[1]
    pos = jnp.arange(t)
    kb, vb, km = jax.vmap(moba_blocks)(kv_rows[:, :, 0], kv_rows[:, :, 1])
    nqc = t // MOBA_Q_CHUNK
    qc = moba_q.reshape(n, nqc, MOBA_Q_CHUNK, MOBA_HEADS, MOBA_HEAD_DIM).swapaxes(0, 1)
    qp = pos.reshape(nqc, MOBA_Q_CHUNK)
    core = jax.vmap(moba_core, in_axes=(0, None, 0, 0, 0))
    out = lax.map(lambda a: core(a[0], a[1], kb, vb, km), (qc, qp))
    return out.swapaxes(0, 1).reshape(n, t, MOBA_HEADS, MOBA_HEAD_DIM)


def attend_sample(q_lat, q_rope, mla_rows, moba_q, kv_rows, cache_mla, cache_kv, page_table, l):
    t = q_lat.shape[1]
    q_pos = PAST_LEN + jnp.arange(t)
    k_pos = jnp.arange(PAST_LEN + t)

    def one(args):
        ql, qr, mrow, mq, kvrow, pt = args
        all_mla = jnp.concatenate([cache_mla[l, pt].reshape(PAST_LEN, MLA_LAT), mrow], axis=0)
        lat = mla_core(ql, qr, all_mla[:, :MLA_KV_LORA], all_mla[:, MLA_KV_LORA:], q_pos, k_pos)
        all_kv = jnp.concatenate([cache_kv[l, pt].reshape(PAST_LEN, 2, MOBA_KV_HEADS, MOBA_HEAD_DIM), kvrow], axis=0)
        kb, vb, km = moba_blocks(all_kv[:, 0], all_kv[:, 1])
        return lat, moba_core(mq, q_pos, kb, vb, km)

    return lax.map(one, (q_lat, q_rope, mla_rows, moba_q, kv_rows, page_table))


def merge_out(lat_a, o_b, ga, gb, w_uv, w_branch_a, w_branch_b, w_out):
    n, t = lat_a.shape[0], lat_a.shape[1]
    o_a = jnp.einsum('nthc,chd->nthd', lat_a, w_uv).reshape(n, t, BRANCH_A_WIDTH)
    br_a = jnp.einsum('ntc,cd->ntd', o_a, w_branch_a)
    br_b = jnp.einsum('ntc,cd->ntd', o_b.reshape(n, t, BRANCH_B_WIDTH), w_branch_b)
    merged = jax.nn.sigmoid(ga) * br_a + jax.nn.sigmoid(gb) * br_b
    return jnp.einsum('ntd,de->nte', merged, w_out)


def moe_ffn(h, l, w_router_group, b_router_group, w_router_expert, b_router_expert, w_exp_up, w_exp_down):
    shp = h.shape
    hf = h.reshape(-1, D_MODEL)
    T = hf.shape[0]
    lg = (hf @ w_router_group[l]).astype(jnp.float32) + b_router_group[l].astype(jnp.float32)
    g_sel = jnp.argmax(lg, axis=-1)
    p_g = jnp.take_along_axis(jax.nn.softmax(lg, axis=-1), g_sel[:, None], axis=-1)
    le = (hf @ w_router_expert[l]).astype(jnp.float32).reshape(T, MOE_GROUPS, MOE_EXPERTS_PER_GROUP) \
        + b_router_expert[l].astype(jnp.float32)
    le_sel = jnp.take_along_axis(le, g_sel[:, None, None], axis=1)[:, 0]
    top_p, top_e = lax.top_k(jax.nn.softmax(le_sel, axis=-1), MOE_TOPK)
    wts = (p_g * top_p / jnp.sum(top_p, axis=-1, keepdims=True)).reshape(-1)
    eid = (g_sel[:, None] * MOE_EXPERTS_PER_GROUP + top_e).reshape(-1).astype(jnp.int32)
    tok = jnp.repeat(jnp.arange(T, dtype=jnp.int32), MOE_TOPK)
    A = T * MOE_TOPK
    order = jnp.argsort(eid)
    eid_s, tok_s, wts_s = eid[order], tok[order], wts[order]
    counts = jnp.bincount(eid, length=MOE_N_EXPERTS)
    padded = (counts + MOE_ROW_BLOCK - 1) // MOE_ROW_BLOCK * MOE_ROW_BLOCK
    pend = jnp.cumsum(padded)
    pstart = pend - padded
    cstart = jnp.cumsum(counts) - counts
    dest = pstart[eid_s] + (jnp.arange(A) - cstart[eid_s])
    n_blocks = (A + MOE_ROW_BLOCK - 1) // MOE_ROW_BLOCK + MOE_N_EXPERTS
    P = n_blocks * MOE_ROW_BLOCK
    src = jnp.full((P,), T, dtype=jnp.int32).at[dest].set(tok_s)
    wp = jnp.zeros((P,), jnp.float32).at[dest].set(wts_s)
    block_exp = jnp.minimum(jnp.searchsorted(pend, jnp.arange(n_blocks) * MOE_ROW_BLOCK, side='right'), MOE_N_EXPERTS - 1)
    hp = jnp.concatenate([hf, jnp.zeros((1, D_MODEL), hf.dtype)], axis=0)

    def run_block(args):
        rows, e = args
        gu = hp[rows] @ w_exp_up[l, e]
        return (jax.nn.silu(gu[:, :MOE_HIDDEN]) * gu[:, MOE_HIDDEN:]) @ w_exp_down[l, e]

    out = lax.map(run_block, (src.reshape(n_blocks, MOE_ROW_BLOCK), block_exp)).reshape(P, D_MODEL)
    y = jnp.zeros((T + 1, D_MODEL), hf.dtype).at[src].add(out * wp[:, None].astype(hf.dtype))[:T]
    return y.reshape(shp)


def setup_inputs(seed: int = 0) -> dict:
    key = jax.random.key(seed)
    ks = jax.random.split(key, 24)
    f32 = jnp.float32

    def nrm(k, shape, scale):
        return jax.random.normal(k, shape, f32) * scale

    n_pages = PAST_LEN // PAGE_SIZE
    n_used = DEC_BATCH * n_pages
    n_pool = n_used + n_used // 4
    page_table = jax.random.permutation(ks[4], n_pool)[:n_used].reshape(DEC_BATCH, n_pages).astype(jnp.int32)
    return {
        "x_prompt": nrm(ks[0], (BATCH, SEQ, D_MODEL), 1.0),
        "x_sample": nrm(ks[1], (DEC_BATCH, DEC_SEQ, D_MODEL), 1.0),
        "cache_mla": nrm(ks[2], (DEPTH, n_pool, PAGE_SIZE, MLA_LAT), 1.0),
        "cache_kv": nrm(ks[3], (DEPTH, n_pool, PAGE_SIZE, 2, MOBA_KV_HEADS, MOBA_HEAD_DIM), 1.0),
        "page_table": page_table,
        "g_attn_norm": 1.0 + nrm(ks[5], (DEPTH, D_MODEL), 0.02),
        "w_in": nrm(ks[6], (DEPTH, D_MODEL, IN_COLS), D_MODEL ** -0.5),
        "g_q_norm": 1.0 + nrm(ks[7], (DEPTH, MLA_Q_LORA), 0.02),
        "g_kv_norm": 1.0 + nrm(ks[8], (DEPTH, MLA_KV_LORA), 0.02),
        "w_uq": nrm(ks[9], (DEPTH, MLA_Q_LORA, MLA_HEADS, MLA_NOPE_DIM + MLA_ROPE_DIM), MLA_Q_LORA ** -0.5),
        "w_uk": nrm(ks[10], (DEPTH, MLA_KV_LORA, MLA_HEADS, MLA_NOPE_DIM), MLA_KV_LORA ** -0.5),
        "w_uv": nrm(ks[11], (DEPTH, MLA_KV_LORA, MLA_HEADS, MLA_V_DIM), MLA_KV_LORA ** -0.5),
        "w_branch_a": nrm(ks[12], (DEPTH, BRANCH_A_WIDTH, D_MODEL), BRANCH_A_WIDTH ** -0.5),
        "w_branch_b": nrm(ks[13], (DEPTH, BRANCH_B_WIDTH, D_MODEL), BRANCH_B_WIDTH ** -0.5),
        "w_out": nrm(ks[14], (DEPTH, D_MODEL, D_MODEL), D_MODEL ** -0.5),
        "g_ffn_norm": 1.0 + nrm(ks[15], (DEPTH, D_MODEL), 0.02),
        "w_router_group": nrm(ks[16], (DEPTH, D_MODEL, MOE_GROUPS), D_MODEL ** -0.5),
        "b_router_group": nrm(ks[17], (DEPTH, MOE_GROUPS), 0.01),
        "w_router_expert": nrm(ks[18], (DEPTH, D_MODEL, MOE_N_EXPERTS), D_MODEL ** -0.5),
        "b_router_expert": nrm(ks[19], (DEPTH, MOE_GROUPS, MOE_EXPERTS_PER_GROUP), 0.01),
        "w_exp_up": nrm(ks[20], (DEPTH, MOE_N_EXPERTS, D_MODEL, 2 * MOE_HIDDEN), D_MODEL ** -0.5),
        "w_exp_down": nrm(ks[21], (DEPTH, MOE_N_EXPERTS, MOE_HIDDEN, D_MODEL), MOE_HIDDEN ** -0.5),
        "g_final": 1.0 + nrm(ks[22], (D_MODEL,), 0.02),
    }


def reference(x_prompt, x_sample, cache_mla, cache_kv, page_table, g_attn_norm, w_in, g_q_norm, g_kv_norm,
              w_uq, w_uk, w_uv, w_branch_a, w_branch_b, w_out, g_ffn_norm, w_router_group, b_router_group,
              w_router_expert, b_router_expert, w_exp_up, w_exp_down, g_final):
    xp, xs = x_prompt, x_sample
    pos_p = jnp.arange(xp.shape[1])
    pos_s = PAST_LEN + jnp.arange(xs.shape[1])
    mla_p_rows, mla_s_rows, kv_p_rows, kv_s_rows = [], [], [], []
    for l in range(DEPTH):
        hp = rms_norm(xp, g_attn_norm[l])
        hs = rms_norm(xs, g_attn_norm[l])
        qlp, qrp, mrp, mqp, kvp, gap, gbp = project_in(hp, pos_p, w_in[l], g_q_norm[l], g_kv_norm[l], w_uq[l], w_uk[l])
        qls, qrs, mrs, mqs, kvs, gas, gbs = project_in(hs, pos_s, w_in[l], g_q_norm[l], g_kv_norm[l], w_uq[l], w_uk[l])
        lat_p = mla_prompt(qlp, qrp, mrp)
        ob_p = moba_prompt(mqp, kvp)
        lat_s, ob_s = attend_sample(qls, qrs, mrs, mqs, kvs, cache_mla, cache_kv, page_table, l)
        xp = xp + merge_out(lat_p, ob_p, gap, gbp, w_uv[l], w_branch_a[l], w_branch_b[l], w_out[l])
        xs = xs + merge_out(lat_s, ob_s, gas, gbs, w_uv[l], w_branch_a[l], w_branch_b[l], w_out[l])
        mla_p_rows.append(mrp)
        mla_s_rows.append(mrs)
        kv_p_rows.append(kvp)
        kv_s_rows.append(kvs)
        xp = xp + moe_ffn(rms_norm(xp, g_ffn_norm[l]), l, w_router_group, b_router_group, w_router_expert,
                          b_router_expert, w_exp_up, w_exp_down)
        xs = xs + moe_ffn(rms_norm(xs, g_ffn_norm[l]), l, w_router_group, b_router_group, w_router_expert,
                          b_router_expert, w_exp_up, w_exp_down)
    y_prompt = rms_norm(xp, g_final)
    y_sample = rms_norm(xs, g_final)
    new_mla_prompt = jnp.stack(mla_p_rows, axis=0)
    new_mla_sample = jnp.stack(mla_s_rows, axis=0)
    new_kv_prompt = jnp.stack(kv_p_rows, axis=0)
    new_kv_sample = jnp.stack(kv_s_rows, axis=0)
    return (y_prompt, y_sample, new_mla_prompt, new_mla_sample, new_kv_prompt, new_kv_sample)
```

```python
import functools

import jax
import jax.numpy as jnp
from jax import lax
from jax.experimental import pallas as pl
from jax.experimental.pallas import tpu as pltpu

F32 = jnp.float32
BF16 = jnp.bfloat16
I32 = jnp.int32

NORM_EPS = 1e-6
NEG = -1e30
ROPE_THETA = 10000.0
ROPE_DIM = 64
ROPE_PAD = 128
MOBA_BLOCK = 256
MOBA_TOPK = 3
MOE_TOPK = 2
MOE_ROW_BLOCK = 128
LANES = 128
VMEM_LIMIT = 56 * 1024 * 1024
NT_DIMS = (((1,), (1,)), ((), ()))


def _params(*sem):
    return pltpu.CompilerParams(dimension_semantics=sem, vmem_limit_bytes=VMEM_LIMIT)


def _tile(m, cands=(512, 256, 128, 64, 32, 16, 8)):
    for c in cands:
        if m % c == 0:
            return c
    raise ValueError(f"no tile for {m}")


def _rms(x, g):
    return x * lax.rsqrt(jnp.mean(x * x, axis=-1, keepdims=True) + NORM_EPS) * g


def _rope128(v, cos, sin):
    rot = pltpu.roll(v, 96, 1) + pltpu.roll(v, 32, 1)
    return v * cos + rot * sin


def _rmsnorm_body(x_ref, g_ref, o_ref):
    o_ref[...] = _rms(x_ref[...].astype(F32), g_ref[...]).astype(o_ref.dtype)


def _rmsnorm(x, g, out_dtype):
    m, d = x.shape
    tm = _tile(m)
    return pl.pallas_call(
        _rmsnorm_body, grid=(m // tm,),
        in_specs=[pl.BlockSpec((tm, d), lambda i: (i, 0)), pl.BlockSpec((1, d), lambda i: (0, 0))],
        out_specs=pl.BlockSpec((tm, d), lambda i: (i, 0)),
        out_shape=jax.ShapeDtypeStruct((m, d), out_dtype),
        compiler_params=_params("parallel"), name="rmsnorm")(x, g.reshape(1, d).astype(F32))


def _mm(a, b, fn, extras, extra_specs, out_shapes, out_specs, tm, tn, name):
    m, k = a.shape
    n = b.shape[1]
    n_ex = len(extras)

    def body(a_ref, b_ref, *refs):
        acc = jnp.dot(a_ref[...], b_ref[...], preferred_element_type=F32)
        res = fn(acc, *[r[...] for r in refs[:n_ex]])
        for o_ref, r in zip(refs[n_ex:], res):
            o_ref[...] = r.astype(o_ref.dtype)

    return pl.pallas_call(
        body, grid=(m // tm, n // tn),
        in_specs=[pl.BlockSpec((tm, k), lambda i, j: (i, 0)), pl.BlockSpec((k, tn), lambda i, j: (0, j))]
        + list(extra_specs),
        out_specs=out_specs, out_shape=out_shapes,
        compiler_params=_params("parallel", "arbitrary"), name=name)(a, b, *extras)


def _ckr_fn(acc, g, cos, sin):
    ckv = _rms(acc[:, :512], g)
    kr = _rope128(acc[:, 512:640], cos, sin)
    rows = jnp.concatenate([ckv, kr], axis=1)
    return rows[:, :576], rows


def _q_body(cq_ref, wn_ref, wr_ref, wuk_ref, cos_ref, sin_ref, q_ref):
    cq = cq_ref[...]
    qn = jnp.dot(cq, wn_ref[...], preferred_element_type=F32)
    ql = jnp.dot(qn.astype(BF16), wuk_ref[...], preferred_element_type=F32)
    qr = jnp.dot(cq, wr_ref[...], preferred_element_type=F32)
    qr = _rope128(qr, cos_ref[...], sin_ref[...])
    q_ref[:, :512] = ql.astype(BF16)
    q_ref[:, 512:] = qr.astype(BF16)


def _q_proj(cq, wn, wr, wuk, cos, sin):
    t, c = cq.shape
    h = wn.shape[0]
    tm = _tile(t)
    return pl.pallas_call(
        _q_body, grid=(t // tm, h),
        in_specs=[pl.BlockSpec((tm, c), lambda i, j: (i, 0)),
                  pl.BlockSpec((None, c, 128), lambda i, j: (j, 0, 0)),
                  pl.BlockSpec((None, c, ROPE_PAD), lambda i, j: (j, 0, 0)),
                  pl.BlockSpec((None, 128, 512), lambda i, j: (j, 0, 0)),
                  pl.BlockSpec((tm, ROPE_PAD), lambda i, j: (i, 0)),
                  pl.BlockSpec((tm, ROPE_PAD), lambda i, j: (i, 0))],
        out_specs=pl.BlockSpec((None, tm, 640), lambda i, j: (j, i, 0)),
        out_shape=jax.ShapeDtypeStruct((h, t, 640), BF16),
        compiler_params=_params("parallel", "arbitrary"), name="q_proj")(cq, wn, wr, wuk, cos, sin)


def _softmax_step(s, v, m_scr, l_scr, acc_scr):
    m_prev = m_scr[...]
    m_new = jnp.maximum(m_prev, jnp.max(s, axis=-1, keepdims=True))
    alpha = jnp.exp(m_prev - m_new)
    p = jnp.exp(s - m_new)
    l_scr[...] = alpha * l_scr[...] + jnp.sum(p, axis=-1, keepdims=True)
    acc_scr[...] = alpha * acc_scr[...] + jnp.dot(p.astype(BF16), v, preferred_element_type=F32)
    m_scr[...] = m_new


def _mla_prompt_body(q_ref, k_ref, wuv_ref, o_ref, m_scr, l_scr, acc_scr, *, tq, tk, heads, scale):
    i = pl.program_id(1)
    j = pl.program_id(2)
    last = (i * tq + tq - 1) // tk

    @pl.when(j == 0)
    def _():
        m_scr[...] = jnp.full_like(m_scr, NEG)
        l_scr[...] = jnp.zeros_like(l_scr)
        acc_scr[...] = jnp.zeros_like(acc_scr)

    @pl.when(j <= last)
    def _():
        q = q_ref[...].reshape(heads * tq, 640)
        k = k_ref[...]
        s = lax.dot_general(q, k, NT_DIMS, preferred_element_type=F32) * scale
        row = lax.broadcasted_iota(I32, s.shape, 0)
        col = lax.broadcasted_iota(I32, s.shape, 1)
        qpos = i * tq + row % tq
        s = jnp.where(j * tk + col <= qpos, s, NEG)
        _softmax_step(s, k[:, :512], m_scr, l_scr, acc_scr)

    @pl.when(j == pl.num_programs(2) - 1)
    def _():
        lat = (acc_scr[...] / l_scr[...]).astype(BF16)
        for h in range(heads):
            o_ref[:, h * 128:(h + 1) * 128] = jnp.dot(
                lat[h * tq:(h + 1) * tq, :], wuv_ref[h], preferred_element_type=F32).astype(o_ref.dtype)


def _mla_prompt(q, kp, wuv, n_seq, seq, scale):
    heads = q.shape[0]
    tq = _tile(seq, (128,))
    tk = _tile(seq, (512, 256, 128))
    nq, nk = seq // tq, seq // tk
    body = functools.partial(_mla_prompt_body, tq=tq, tk=tk, heads=heads, scale=scale)
    return pl.pallas_call(
        body, grid=(n_seq, nq, nk),
        in_specs=[pl.BlockSpec((heads, tq, 640), lambda n, i, j: (0, n * nq + i, 0)),
                  pl.BlockSpec((tk, 640), lambda n, i, j: (n * nk + jnp.minimum(j, (i * tq + tq - 1) // tk), 0)),
                  pl.BlockSpec((heads, 512, 128), lambda n, i, j: (0, 0, 0))],
        out_specs=pl.BlockSpec((tq, heads * 128), lambda n, i, j: (n * nq + i, 0)),
        out_shape=jax.ShapeDtypeStruct((n_seq * seq, heads * 128), BF16),
        scratch_shapes=[pltpu.VMEM((heads * tq, 1), F32), pltpu.VMEM((heads * tq, 1), F32),
                        pltpu.VMEM((heads * tq, 512), F32)],
        compiler_params=_params("parallel", "parallel", "arbitrary"), name="mla_prompt")(q, kp, wuv)


def _mla_sample_body(pt_ref, q_ref, knew_ref, *refs, n_par, page, n_new, scale):
    page_refs = refs[:n_par]
    o_ref, kbuf, m_scr, l_scr, acc_scr = refs[n_par:]
    s_idx = pl.program_id(1)

    @pl.when(s_idx == 0)
    def _():
        m_scr[...] = jnp.full_like(m_scr, NEG)
        l_scr[...] = jnp.zeros_like(l_scr)
        acc_scr[...] = jnp.zeros_like(acc_scr)
        kbuf[:, 512:] = jnp.zeros((kbuf.shape[0], 128), BF16)

    for k in range(n_par):
        kbuf[k * page:(k + 1) * page, :576] = page_refs[k][...].astype(BF16)
    q = q_ref[...]
    kb = kbuf[...]
    s = lax.dot_general(q, kb, NT_DIMS, preferred_element_type=F32) * scale
    _softmax_step(s, kb[:, :512], m_scr, l_scr, acc_scr)

    @pl.when(s_idx == pl.num_programs(1) - 1)
    def _():
        kn = knew_ref[...]
        sn = lax.dot_general(q, kn, NT_DIMS, preferred_element_type=F32) * scale
        row = lax.broadcasted_iota(I32, sn.shape, 0)
        col = lax.broadcasted_iota(I32, sn.shape, 1)
        sn = jnp.where(col <= row % n_new, sn, NEG)
        _softmax_step(sn, kn[:, :512], m_scr, l_scr, acc_scr)
        o_ref[...] = (acc_scr[...] / l_scr[...]).astype(o_ref.dtype)


def _mla_sample(page_table, q, knew, cache, layer, n_new, scale, n_par):
    b, rows, _ = q.shape
    n_pages = page_table.shape[1]
    page, width = cache.shape[2], cache.shape[3]
    assert n_pages % n_par == 0 and n_new <= LANES
    body = functools.partial(_mla_sample_body, n_par=n_par, page=page, n_new=n_new, scale=scale)
    page_specs = [
        pl.BlockSpec((None, None, page, width),
                     functools.partial(lambda bi, si, pt, k: (layer, pt[bi, si * n_par + k], 0, 0), k=k))
        for k in range(n_par)]
    grid_spec = pltpu.PrefetchScalarGridSpec(
        num_scalar_prefetch=1, grid=(b, n_pages // n_par),
        in_specs=[pl.BlockSpec((None, rows, 640), lambda bi, si, pt: (bi, 0, 0)),
                  pl.BlockSpec((None, LANES, 640), lambda bi, si, pt: (bi, 0, 0))] + page_specs,
        out_specs=pl.BlockSpec((None, rows, 512), lambda bi, si, pt: (bi, 0, 0)),
        scratch_shapes=[pltpu.VMEM((n_par * page, 640), BF16), pltpu.VMEM((rows, 1), F32),
                        pltpu.VMEM((rows, 1), F32), pltpu.VMEM((rows, 512), F32)])
    return pl.pallas_call(
        body, grid_spec=grid_spec, out_shape=jax.ShapeDtypeStruct((b, rows, 512), BF16),
        compiler_params=_params("parallel", "arbitrary"), name="mla_sample")(
            page_table, q, knew, *([cache] * n_par))


def _uv_body(lat_ref, w_ref, o_ref):
    o_ref[...] = jnp.dot(lat_ref[...], w_ref[...], preferred_element_type=F32).astype(o_ref.dtype)


def _uv_proj(lat, wuv):
    h, t, c = lat.shape
    return pl.pallas_call(
        _uv_body, grid=(h,),
        in_specs=[pl.BlockSpec((None, t, c), lambda j: (j, 0, 0)), pl.BlockSpec((None, c, 128), lambda j: (j, 0, 0))],
        out_specs=pl.BlockSpec((t, 128), lambda j: (0, j)),
        out_shape=jax.ShapeDtypeStruct((t, h * 128), BF16),
        compiler_params=_params("parallel"), name="uv_proj")(lat, wuv)


def _top_mask(gate, valid, topk):
    lane = lax.broadcasted_iota(I32, gate.shape, 1)
    g = jnp.where(valid, gate, NEG)
    sel = jnp.zeros(gate.shape, jnp.bool_)
    for _ in range(topk):
        mx = jnp.max(g, axis=-1, keepdims=True)
        idx = jnp.min(jnp.where(g == mx, lane, gate.shape[1]), axis=-1, keepdims=True)
        hit = lane == idx
        sel = sel | (hit & (mx > 0.5 * NEG))
        g = jnp.where(hit, -jnp.inf, g)
    return sel


def _kmean_body(k_ref, o_ref, *, nb):
    k = k_ref[...]
    w = k.shape[1]
    km = jnp.sum(k.reshape(nb, MOBA_BLOCK, w), axis=1) * (1.0 / MOBA_BLOCK)
    o_ref[...] = jnp.zeros_like(o_ref)
    o_ref[:nb, :] = km


def _kmean_prompt(kv, n_seq, seq, kw):
    nb = seq // MOBA_BLOCK
    assert nb <= LANES
    return pl.pallas_call(
        functools.partial(_kmean_body, nb=nb), grid=(n_seq,),
        in_specs=[pl.BlockSpec((seq, kw), lambda n: (n, 0))],
        out_specs=pl.BlockSpec((LANES, kw), lambda n: (n, 0)),
        out_shape=jax.ShapeDtypeStruct((n_seq * LANES, kw), F32),
        compiler_params=_params("parallel"), name="moba_kmean")(kv)


def _moba_prompt_body(q_ref, k_ref, v_ref, km_ref, o_ref, m_scr, l_scr, acc_scr, *, group, kv_heads, nb, scale):
    g = pl.program_id(1)
    i = pl.program_id(2)
    blk = MOBA_BLOCK
    rows = group * blk
    qv = q_ref[...]
    q = jnp.concatenate([qv[:, h * 128:(h + 1) * 128] for h in range(group)], axis=0)
    gate = lax.dot_general(q, km_ref[...].astype(BF16), NT_DIMS, preferred_element_type=F32)
    lane = lax.broadcasted_iota(I32, gate.shape, 1)
    sel = _top_mask(gate, lane < i, MOBA_TOPK)
    self = jnp.where(sel, 1.0, 0.0)

    row1 = lax.broadcasted_iota(I32, (rows, 1), 0)
    head = (g * group + row1 // blk + 1).astype(F32)
    slope = jnp.exp2(-8.0 * head / (group * kv_heads))
    row = lax.broadcasted_iota(I32, (rows, blk), 0)
    col = lax.broadcasted_iota(I32, (rows, blk), 1)
    rel = row % blk - col

    kd = k_ref[pl.ds(pl.multiple_of(i * blk, blk), blk), :]
    vd = v_ref[pl.ds(pl.multiple_of(i * blk, blk), blk), :]
    s = lax.dot_general(q, kd, NT_DIMS, preferred_element_type=F32) * scale - slope * rel.astype(F32)
    s = jnp.where(rel >= 0, s, NEG)
    m0 = jnp.max(s, axis=-1, keepdims=True)
    p = jnp.exp(s - m0)
    m_scr[...] = m0
    l_scr[...] = jnp.sum(p, axis=-1, keepdims=True)
    acc_scr[...] = jnp.dot(p.astype(BF16), vd, preferred_element_type=F32)

    for j in range(nb - 1):
        @pl.when(j < i)
        def _(j=j):
            kj = k_ref[j * blk:(j + 1) * blk, :]
            vj = v_ref[j * blk:(j + 1) * blk, :]
            dist = ((i - j) * blk + rel).astype(F32)
            sj = lax.dot_general(q, kj, NT_DIMS, preferred_element_type=F32) * scale - slope * dist
            sj = jnp.where(self[:, j:j + 1] > 0.5, sj, NEG)
            _softmax_step(sj, vj, m_scr, l_scr, acc_scr)

    o = acc_scr[...] / l_scr[...]
    for h in range(group):
        o_ref[:, h * 128:(h + 1) * 128] = o[h * blk:(h + 1) * blk, :].astype(o_ref.dtype)


def _moba_prompt(q, kvb, km, n_seq, seq, kv_heads, scale):
    heads = q.shape[1] // 128
    group = heads // kv_heads
    nb = seq // MOBA_BLOCK
    rows = group * MOBA_BLOCK
    body = functools.partial(_moba_prompt_body, group=group, kv_heads=kv_heads, nb=nb, scale=scale)
    return pl.pallas_call(
        body, grid=(n_seq, kv_heads, nb),
        in_specs=[pl.BlockSpec((MOBA_BLOCK, group * 128), lambda n, g, i: (n * nb + i, g)),
                  pl.BlockSpec((seq, 128), lambda n, g, i: (n, g)),
                  pl.BlockSpec((seq, 128), lambda n, g, i: (n, kv_heads + g)),
                  pl.BlockSpec((LANES, 128), lambda n, g, i: (n, g))],
        out_specs=pl.BlockSpec((MOBA_BLOCK, group * 128), lambda n, g, i: (n * nb + i, g)),
        out_shape=jax.ShapeDtypeStruct((n_seq * seq, q.shape[1]), BF16),
        scratch_shapes=[pltpu.VMEM((rows, 1), F32), pltpu.VMEM((rows, 1), F32), pltpu.VMEM((rows, 128), F32)],
        compiler_params=_params("parallel", "parallel", "arbitrary"), name="moba_prompt")(q, kvb, kvb, km)


def _moba_sample_body(pt_ref, q_ref, new_ref, *refs, n_par, page, kv_heads, group, n_new, past, scale):
    page_refs = refs[:n_par]
    o_ref, gate_scr, m_scr, l_scr, o_scr = refs[n_par:]
    s_idx = pl.program_id(1)
    ppb = MOBA_BLOCK // page
    bps = n_par // ppb
    nbp = past // MOBA_BLOCK
    rows = group * n_new
    lane = lax.broadcasted_iota(I32, (rows, LANES), 1)
    row1 = lax.broadcasted_iota(I32, (rows, 1), 0)
    t_q = row1 % n_new
    col = lax.broadcasted_iota(I32, (rows, MOBA_BLOCK), 1)

    def slope_of(g):
        head = (g * group + row1 // n_new + 1).astype(F32)
        return jnp.exp2(-8.0 * head / (group * kv_heads))

    @pl.when(s_idx == 0)
    def _():
        gate_scr[...] = jnp.full_like(gate_scr, NEG)
        m_scr[...] = jnp.full_like(m_scr, NEG)
        l_scr[...] = jnp.zeros_like(l_scr)

    for bb in range(bps):
        blk = s_idx * bps + bb
        for g in range(kv_heads):
            kf = jnp.concatenate([page_refs[bb * ppb + u][:, g * 128:(g + 1) * 128] for u in range(ppb)], axis=0)
            vf = jnp.concatenate(
                [page_refs[bb * ppb + u][:, (kv_heads + g) * 128:(kv_heads + g + 1) * 128] for u in range(ppb)], axis=0)
            q = q_ref[g]
            kmean = jnp.sum(kf, axis=0, keepdims=True) * (1.0 / MOBA_BLOCK)
            gate = jnp.sum(q.astype(F32) * kmean, axis=-1, keepdims=True)
            dist = (past + t_q - blk * MOBA_BLOCK - col).astype(F32)
            sc = lax.dot_general(q, kf.astype(BF16), NT_DIMS, preferred_element_type=F32) * scale - slope_of(g) * dist
            mb = jnp.max(sc, axis=-1, keepdims=True)
            p = jnp.exp(sc - mb)
            hit = lane == blk
            gate_scr[g] = jnp.where(hit, gate, gate_scr[g])
            m_scr[g] = jnp.where(hit, mb, m_scr[g])
            l_scr[g] = jnp.where(hit, jnp.sum(p, axis=-1, keepdims=True), l_scr[g])
            o_scr[g, blk] = jnp.dot(p.astype(BF16), vf.astype(BF16), preferred_element_type=F32)

    @pl.when(s_idx == pl.num_programs(1) - 1)
    def _():
        for g in range(kv_heads):
            q = q_ref[g]
            kn = new_ref[g]
            vn = new_ref[kv_heads + g]
            coln = lax.broadcasted_iota(I32, (rows, LANES), 1)
            sn = lax.dot_general(q, kn, NT_DIMS, preferred_element_type=F32) * scale \
                - slope_of(g) * (t_q - coln).astype(F32)
            sn = jnp.where(coln <= t_q, sn, NEG)
            m_own = jnp.max(sn, axis=-1, keepdims=True)
            p_own = jnp.exp(sn - m_own)
            l_own = jnp.sum(p_own, axis=-1, keepdims=True)
            o_own = jnp.dot(p_own.astype(BF16), vn, preferred_element_type=F32)

            sel = _top_mask(gate_scr[g], lane < nbp, MOBA_TOPK)
            mb = jnp.where(sel, m_scr[g], NEG)
            m_all = jnp.maximum(m_own, jnp.max(mb, axis=-1, keepdims=True))
            w = jnp.where(sel, jnp.exp(mb - m_all), 0.0)
            w_own = jnp.exp(m_own - m_all)
            l_all = w_own * l_own + jnp.sum(w * l_scr[g], axis=-1, keepdims=True)
            o_all = w_own * o_own
            for j in range(nbp):
                o_all = o_all + w[:, j:j + 1] * o_scr[g, j]
            o_ref[g] = (o_all / l_all).astype(o_ref.dtype)


def _moba_sample(page_table, q, new, cache, layer, n_new, past, scale, n_par):
    b, kv_heads, rows, _ = q.shape
    group = rows // n_new
    n_pages = page_table.shape[1]
    page, width = cache.shape[2], cache.shape[3]
    nbp = past // MOBA_BLOCK
    assert MOBA_BLOCK % page == 0 and n_par % (MOBA_BLOCK // page) == 0 and n_pages % n_par == 0
    assert past % MOBA_BLOCK == 0 and nbp <= LANES and n_new <= MOBA_BLOCK and n_new <= LANES
    body = functools.partial(_moba_sample_body, n_par=n_par, page=page, kv_heads=kv_heads, group=group,
                             n_new=n_new, past=past, scale=scale)
    page_specs = [
        pl.BlockSpec((None, None, page, width),
                     functools.partial(lambda bi, si, pt, k: (layer, pt[bi, si * n_par + k], 0, 0), k=k))
        for k in range(n_par)]
    grid_spec = pltpu.PrefetchScalarGridSpec(
        num_scalar_prefetch=1, grid=(b, n_pages // n_par),
        in_specs=[pl.BlockSpec((None, kv_heads, rows, 128), lambda bi, si, pt: (bi, 0, 0, 0)),
                  pl.BlockSpec((None, 2 * kv_heads, LANES, 128), lambda bi, si, pt: (bi, 0, 0, 0))] + page_specs,
        out_specs=pl.BlockSpec((None, kv_heads, rows, 128), lambda bi, si, pt: (bi, 0, 0, 0)),
        scratch_shapes=[pltpu.VMEM((kv_heads, rows, LANES), F32), pltpu.VMEM((kv_heads, rows, LANES), F32),
                        pltpu.VMEM((kv_heads, rows, LANES), F32), pltpu.VMEM((kv_heads, nbp, rows, 128), F32)])
    return pl.pallas_call(
        body, grid_spec=grid_spec, out_shape=jax.ShapeDtypeStruct(q.shape, BF16),
        compiler_params=_params("parallel", "arbitrary"), name="moba_sample")(
            page_table, q, new, *([cache] * n_par))


def _merge_body(oa_ref, ob_ref, wa_ref, wb_ref, ga_ref, gb_ref, o_ref):
    bra = jnp.dot(oa_ref[...], wa_ref[...], preferred_element_type=F32)
    brb = jnp.dot(ob_ref[...], wb_ref[...], preferred_element_type=F32)
    o_ref[...] = (ga_ref[...].astype(F32) * bra + gb_ref[...].astype(F32) * brb).astype(o_ref.dtype)


def _merge(oa, ob, wa, wb, gates):
    t, ka = oa.shape
    kb = ob.shape[1]
    d = wa.shape[1]
    tm, tn = _tile(t), _tile(d)
    nj = d // tn
    return pl.pallas_call(
        _merge_body, grid=(t // tm, nj),
        in_specs=[pl.BlockSpec((tm, ka), lambda i, j: (i, 0)), pl.BlockSpec((tm, kb), lambda i, j: (i, 0)),
                  pl.BlockSpec((ka, tn), lambda i, j: (0, j)), pl.BlockSpec((kb, tn), lambda i, j: (0, j)),
                  pl.BlockSpec((tm, tn), lambda i, j: (i, j)), pl.BlockSpec((tm, tn), lambda i, j: (i, nj + j))],
        out_specs=pl.BlockSpec((tm, tn), lambda i, j: (i, j)),
        out_shape=jax.ShapeDtypeStruct((t, d), BF16),
        compiler_params=_params("parallel", "arbitrary"), name="merge")(oa, ob, wa, wb, gates, gates)


def _router_body(x_ref, g_ref, w_ref, b_ref, h_ref, id_ref, wt_ref, *, n_groups, per_group):
    h = _rms(x_ref[...], g_ref[...])
    h_ref[...] = h
    logits = jnp.dot(h, w_ref[...], preferred_element_type=F32, precision=lax.Precision.HIGHEST) + b_ref[...]
    lane = lax.broadcasted_iota(I32, logits.shape, 1)
    lg = jnp.where(lane < n_groups, logits, NEG)
    mg = jnp.max(lg, axis=-1, keepdims=True)
    g_sel = jnp.min(jnp.where(lg == mg, lane, LANES), axis=-1, keepdims=True)
    p_g = 1.0 / jnp.sum(jnp.exp(lg - mg), axis=-1, keepdims=True)
    e_lane = lane - n_groups
    in_group = (e_lane >= g_sel * per_group) & (e_lane < (g_sel + 1) * per_group)
    le = jnp.where(in_group, logits, NEG)
    m1 = jnp.max(le, axis=-1, keepdims=True)
    i1 = jnp.min(jnp.where(le == m1, lane, LANES), axis=-1, keepdims=True)
    le2 = jnp.where(lane == i1, NEG, le)
    m2 = jnp.max(le2, axis=-1, keepdims=True)
    i2 = jnp.min(jnp.where(le2 == m2, lane, LANES), axis=-1, keepdims=True)
    e2 = jnp.exp(m2 - m1)
    w1 = p_g / (1.0 + e2)
    w2 = p_g * e2 / (1.0 + e2)
    id_ref[...] = jnp.where(lane == 0, i1 - n_groups, jnp.where(lane == 1, i2 - n_groups, 0))
    wt_ref[...] = jnp.where(lane == 0, w1, jnp.where(lane == 1, w2, 0.0))


def _router(x, g, w, b, n_groups, per_group):
    t, d = x.shape
    tm = _tile(t, (256, 128, 64, 32, 16, 8))
    body = functools.partial(_router_body, n_groups=n_groups, per_group=per_group)
    return pl.pallas_call(
        body, grid=(t // tm,),
        in_specs=[pl.BlockSpec((tm, d), lambda i: (i, 0)), pl.BlockSpec((1, d), lambda i: (0, 0)),
                  pl.BlockSpec((d, LANES), lambda i: (0, 0)), pl.BlockSpec((1, LANES), lambda i: (0, 0))],
        out_specs=[pl.BlockSpec((tm, d), lambda i: (i, 0)), pl.BlockSpec((tm, LANES), lambda i: (i, 0)),
                   pl.BlockSpec((tm, LANES), lambda i: (i, 0))],
        out_shape=[jax.ShapeDtypeStruct((t, d), F32), jax.ShapeDtypeStruct((t, LANES), I32),
                   jax.ShapeDtypeStruct((t, LANES), F32)],
        compiler_params=_params("parallel"), name="ffn_norm_router")(x, g, w, b)


def _expert_body(src_ref, dst_ref, bexp_ref, nvalid_ref, h_hbm, wup_ref, wdn_ref, wp_ref, y_hbm,
                 xbuf, obuf, sem, *, rb, hidden):
    b = pl.program_id(0)

    def row_in(r):
        return pltpu.make_async_copy(h_hbm.at[pl.ds(src_ref[b * rb + r], 1), :], xbuf.at[pl.ds(r, 1), :], sem.at[0])

    def row_out(r):
        return pltpu.make_async_copy(obuf.at[pl.ds(r, 1), :], y_hbm.at[pl.ds(dst_ref[b * rb + r], 1), :], sem.at[1])

    @pl.when(b < nvalid_ref[0])
    def _():
        def start_in(r, c):
            row_in(r).start()
            return c
        lax.fori_loop(0, rb, start_in, 0)

        def wait_in(r, c):
            row_in(r).wait()
            return c
        lax.fori_loop(0, rb, wait_in, 0)

        x = xbuf[...].astype(BF16)
        gu = jnp.dot(x, wup_ref[...], preferred_element_type=F32)
        gate, up = gu[:, :hidden], gu[:, hidden:]
        act = gate * (1.0 / (1.0 + jnp.exp(-gate))) * up
        out = jnp.dot(act.astype(BF16), wdn_ref[...], preferred_element_type=F32)
        obuf[...] = out * wp_ref[...]

        def start_out(r, c):
            @pl.when(dst_ref[b * rb + r] >= 0)
            def _():
                row_out(r).start()
            return c
        lax.fori_loop(0, rb, start_out, 0)

        def wait_out(r, c):
            @pl.when(dst_ref[b * rb + r] >= 0)
            def _():
                row_out(r).wait()
            return c
        lax.fori_loop(0, rb, wait_out, 0)


def _experts(src, dst, bexp, nvalid, h, wup, wdn, wp, n_out_rows, rb):
    p = src.shape[0]
    n_blocks = p // rb
    d = h.shape[1]
    hidden = wdn.shape[1]
    grid_spec = pltpu.PrefetchScalarGridSpec(
        num_scalar_prefetch=4, grid=(n_blocks,),
        in_specs=[pl.BlockSpec(memory_space=pl.ANY),
                  pl.BlockSpec((None, d, 2 * hidden), lambda i, s, t, e, n: (e[i], 0, 0)),
                  pl.BlockSpec((None, hidden, d), lambda i, s, t, e, n: (e[i], 0, 0)),
                  pl.BlockSpec((rb, 1), lambda i, s, t, e, n: (i, 0))],
        out_specs=pl.BlockSpec(memory_space=pl.ANY),
        scratch_shapes=[pltpu.VMEM((rb, d), F32), pltpu.VMEM((rb, d), F32), pltpu.SemaphoreType.DMA((2,))])
    return pl.pallas_call(
        functools.partial(_expert_body, rb=rb, hidden=hidden), grid_spec=grid_spec,
        out_shape=jax.ShapeDtypeStruct((n_out_rows, d), F32),
        compiler_params=_params("arbitrary"), name="moe_experts")(src, dst, bexp, nvalid, h, wup, wdn, wp)


def _final_body(x_ref, y0_ref, y1_ref, g_ref, o_ref, *, norm):
    x = x_ref[...] + y0_ref[...] + y1_ref[...]
    o_ref[...] = _rms(x, g_ref[...]) if norm else x


def _final(x, y2, g, norm):
    t, d = x.shape
    tm = _tile(t, (256, 128, 64, 32, 16, 8))
    return pl.pallas_call(
        functools.partial(_final_body, norm=norm), grid=(t // tm,),
        in_specs=[pl.BlockSpec((tm, d), lambda i: (i, 0)), pl.BlockSpec((None, tm, d), lambda i: (0, i, 0)),
                  pl.BlockSpec((None, tm, d), lambda i: (1, i, 0)), pl.BlockSpec((1, d), lambda i: (0, 0))],
        out_specs=pl.BlockSpec((tm, d), lambda i: (i, 0)),
        out_shape=jax.ShapeDtypeStruct((t, d), F32),
        compiler_params=_params("parallel"), name="final_norm")(x, y2, y2, g)


def _dispatch(ids, wts, t, n_experts, rb):
    eid = ids.reshape(-1)
    a = eid.shape[0]
    order = jnp.argsort(eid, stable=True)
    eid_s = eid[order]
    counts = jnp.bincount(eid, length=n_experts)
    padded = (counts + rb - 1) // rb * rb
    pend = jnp.cumsum(padded)
    pstart = pend - padded
    cstart = jnp.cumsum(counts) - counts
    dest = pstart[eid_s] + (jnp.arange(a) - cstart[eid_s])
    n_blocks = (a + rb - 1) // rb + n_experts
    p = n_blocks * rb
    tok = (order // MOE_TOPK).astype(I32)
    slot = (order % MOE_TOPK).astype(I32)
    src = jnp.full((p,), t, I32).at[dest].set(tok)
    slot_p = jnp.zeros((p,), I32).at[dest].set(slot)
    wp = jnp.zeros((p,), F32).at[dest].set(wts.reshape(-1)[order])
    bexp = jnp.minimum(jnp.searchsorted(pend, jnp.arange(n_blocks) * rb, side='right'), n_experts - 1).astype(I32)
    nvalid = (pend[-1] // rb).astype(I32).reshape(1)
    return src, slot_p, wp, bexp, nvalid


def kernel(x_prompt, x_sample, cache_mla, cache_kv, page_table, g_attn_norm, w_in, g_q_norm, g_kv_norm, w_uq, w_uk,
           w_uv, w_branch_a, w_branch_b, w_out, g_ffn_norm, w_router_group, b_router_group, w_router_expert,
           b_router_expert, w_exp_up, w_exp_down, g_final):
    n_p, seq, d = x_prompt.shape
    n_s, n_new, _ = x_sample.shape
    depth = w_in.shape[0]
    n_pages, page = page_table.shape[1], cache_mla.shape[2]
    past = n_pages * page
    q_lora, heads_a = w_uq.shape[1], w_uq.shape[2]
    nope = w_uk.shape[3]
    kv_lora = w_uk.shape[1]
    lat_w = cache_mla.shape[3]
    kv_heads, hd = cache_kv.shape[4], cache_kv.shape[5]
    heads_b = w_branch_b.shape[1] // hd
    group = heads_b // kv_heads
    n_groups, per_group = b_router_expert.shape[1], b_router_expert.shape[2]
    n_experts = n_groups * per_group
    assert (kv_lora, nope, lat_w - kv_lora, hd, w_uv.shape[3]) == (512, 128, ROPE_DIM, 128, 128)
    assert n_groups + n_experts <= LANES
    tp, ts = n_p * seq, n_s * n_new
    t = tp + ts
    mla_scale = (nope + ROPE_DIM) ** -0.5
    moba_scale = hd ** -0.5
    kvw = kv_heads * hd

    half = ROPE_DIM // 2
    inv = ROPE_THETA ** (-jnp.arange(half, dtype=F32) / half)
    pos = jnp.concatenate([jnp.tile(jnp.arange(seq), n_p), jnp.tile(past + jnp.arange(n_new), n_s)])
    ang = pos.astype(F32)[:, None] * inv[None, :]
    zpad = jnp.zeros((t, ROPE_PAD - ROPE_DIM), F32)
    cos_t = jnp.concatenate([jnp.cos(ang), jnp.cos(ang), zpad], axis=1)
    sin_t = jnp.concatenate([-jnp.sin(ang), jnp.sin(ang), zpad], axis=1)

    x = jnp.concatenate([x_prompt.reshape(tp, d), x_sample.reshape(ts, d)], axis=0)
    new_mla_p, new_mla_s, new_kv_p, new_kv_s = [], [], [], []
    tm = _tile(t)
    for l in range(depth):
        o0 = q_lora
        o1 = o0 + lat_w
        o2 = o1 + heads_b * hd
        o3 = o2 + 2 * kvw
        wl = w_in[l]
        w_cq = wl[:, :o0].astype(BF16)
        w_ckr = jnp.pad(wl[:, o0:o1], ((0, 0), (0, 640 - lat_w))).astype(BF16)
        w_qb = wl[:, o1:o2].astype(BF16)
        w_kv = wl[:, o2:o3].astype(BF16)
        w_g = wl[:, o3:].astype(BF16)

        h = _rmsnorm(x, g_attn_norm[l], BF16)
        row_spec = lambda w: pl.BlockSpec((tm, w), lambda i, j: (i, 0))
        one_spec = lambda w: pl.BlockSpec((1, w), lambda i, j: (0, 0))
        (cq,) = _mm(h, w_cq, lambda acc, g: (_rms(acc, g),), [g_q_norm[l].reshape(1, -1)], [one_spec(q_lora)],
                    [jax.ShapeDtypeStruct((t, q_lora), BF16)], [pl.BlockSpec((tm, q_lora), lambda i, j: (i, 0))],
                    tm, q_lora, "in_proj_cq")
        mla_rows, kp = _mm(h, w_ckr, _ckr_fn, [g_kv_norm[l].reshape(1, -1), cos_t, sin_t],
                           [one_spec(kv_lora), row_spec(ROPE_PAD), row_spec(ROPE_PAD)],
                           [jax.ShapeDtypeStruct((t, lat_w), F32), jax.ShapeDtypeStruct((t, 640), BF16)],
                           [pl.BlockSpec((tm, lat_w), lambda i, j: (i, 0)), pl.BlockSpec((tm, 640), lambda i, j: (i, 0))],
                           tm, 640, "in_proj_ckr")
        tn_q = _tile(heads_b * hd, (1024, 512, 256, 128))
        (qb,) = _mm(h, w_qb, lambda acc: (acc,), [], [], [jax.ShapeDtypeStruct((t, heads_b * hd), BF16)],
                    [pl.BlockSpec((tm, tn_q), lambda i, j: (i, j))], tm, tn_q, "in_proj_qb")
        kv_rows, kvb = _mm(h, w_kv, lambda acc: (acc, acc), [], [],
                           [jax.ShapeDtypeStruct((t, 2 * kvw), F32), jax.ShapeDtypeStruct((t, 2 * kvw), BF16)],
                           [pl.BlockSpec((tm, 2 * kvw), lambda i, j: (i, 0))] * 2, tm, 2 * kvw, "in_proj_kv")
        tn_g = _tile(2 * d, (1024, 512, 256, 128))
        (gates,) = _mm(h, w_g, lambda acc: (1.0 / (1.0 + jnp.exp(-acc)),), [], [],
                       [jax.ShapeDtypeStruct((t, 2 * d), BF16)], [pl.BlockSpec((tm, tn_g), lambda i, j: (i, j))],
                       tm, tn_g, "in_proj_gates")

        wq = w_uq[l].transpose(1, 0, 2)
        wn = wq[:, :, :nope].astype(BF16)
        wr = jnp.pad(wq[:, :, nope:], ((0, 0), (0, 0), (0, ROPE_PAD - ROPE_DIM))).astype(BF16)
        wuk = w_uk[l].transpose(1, 2, 0).astype(BF16)
        wuv = w_uv[l].transpose(1, 0, 2).astype(BF16)
        qa = _q_proj(cq, wn, wr, wuk, cos_t, sin_t)

        oa_p = _mla_prompt(qa, kp, wuv, n_p, seq, mla_scale)
        km = _kmean_prompt(kv_rows, n_p, seq, kvw)
        ob_p = _moba_prompt(qb, kvb, km, n_p, seq, kv_heads, moba_scale)

        rows_a = heads_a * n_new
        qa_s = qa[:, tp:, :].reshape(heads_a, n_s, n_new, 640).transpose(1, 0, 2, 3).reshape(n_s, rows_a, 640)
        knew = jnp.pad(kp[tp:].reshape(n_s, n_new, 640), ((0, 0), (0, LANES - n_new), (0, 0)))
        lat_s = _mla_sample(page_table, qa_s, knew, cache_mla, l, n_new, mla_scale, n_par=8)
        lat_s = lat_s.reshape(n_s, heads_a, n_new, kv_lora).transpose(1, 0, 2, 3).reshape(heads_a, ts, kv_lora)
        oa_s = _uv_proj(lat_s, wuv)

        qb_s = qb[tp:].reshape(n_s, n_new, kv_heads, group, hd).transpose(0, 2, 3, 1, 4)
        qb_s = qb_s.reshape(n_s, kv_heads, group * n_new, hd)
        new_b = kvb[tp:].reshape(n_s, n_new, 2 * kv_heads, hd).transpose(0, 2, 1, 3)
        new_b = jnp.pad(new_b, ((0, 0), (0, 0), (0, LANES - n_new), (0, 0)))
        ckv = cache_kv.reshape(depth, cache_kv.shape[1], page, 2 * kvw)
        ob_s = _moba_sample(page_table, qb_s, new_b, ckv, l, n_new, past, moba_scale, n_par=8)
        ob_s = ob_s.reshape(n_s, kv_heads, group, n_new, hd).transpose(0, 3, 1, 2, 4).reshape(ts, heads_b * hd)

        oa = jnp.concatenate([oa_p, oa_s], axis=0)
        ob = jnp.concatenate([ob_p, ob_s], axis=0)
        merged = _merge(oa, ob, w_branch_a[l].astype(BF16), w_branch_b[l].astype(BF16), gates)
        tn_o = _tile(d, (1024, 512, 256, 128))
        (x,) = _mm(merged, w_out[l].astype(BF16), lambda acc, xr: (acc + xr,), [x],
                   [pl.BlockSpec((tm, tn_o), lambda i, j: (i, j))], [jax.ShapeDtypeStruct((t, d), F32)],
                   [pl.BlockSpec((tm, tn_o), lambda i, j: (i, j))], tm, tn_o, "out_proj")

        new_mla_p.append(mla_rows[:tp].reshape(n_p, seq, lat_w))
        new_mla_s.append(mla_rows[tp:].reshape(n_s, n_new, lat_w))
        new_kv_p.append(kv_rows[:tp].reshape(n_p, seq, 2, kv_heads, hd))
        new_kv_s.append(kv_rows[tp:].reshape(n_s, n_new, 2, kv_heads, hd))

        w_r = jnp.concatenate([w_router_group[l], w_router_expert[l],
                               jnp.zeros((d, LANES - n_groups - n_experts), F32)], axis=1)
        b_r = jnp.concatenate([b_router_group[l], b_router_expert[l].reshape(-1),
                               jnp.zeros((LANES - n_groups - n_experts,), F32)]).reshape(1, LANES)
        h2, ids, wts = _router(x, g_ffn_norm[l].reshape(1, d), w_r, b_r, n_groups, per_group)
        rb = MOE_ROW_BLOCK
        src, slot, wp, bexp, nvalid = _dispatch(ids[:, :MOE_TOPK], wts[:, :MOE_TOPK], t, n_experts, rb)
        is_pad = src == t
        gsrc = jnp.where(is_pad, 0, src)
        dst = jnp.where(is_pad, -1, slot * t + src)
        y2 = _experts(gsrc, dst, bexp, nvalid, h2, w_exp_up[l].astype(BF16), w_exp_down[l].astype(BF16),
                      wp.reshape(-1, 1), MOE_TOPK * t, rb)
        y2 = y2.reshape(MOE_TOPK, t, d)
        last = l + 1 == depth
        x = _final(x, y2, g_final.reshape(1, d) if last else jnp.ones((1, d), F32), last)
    y = x
    return (y[:tp].reshape(n_p, seq, d), y[tp:].reshape(n_s, n_new, d),
            jnp.stack(new_mla_p), jnp.stack(new_mla_s), jnp.stack(new_kv_p), jnp.stack(new_kv_s))
```

```python
import functools

import jax
import jax.numpy as jnp
from jax import lax
from jax.experimental import pallas as pl
from jax.experimental.pallas import tpu as pltpu

F32 = jnp.float32
BF16 = jnp.bfloat16
I32 = jnp.int32

NORM_EPS = 1e-6
NEG = -1e30
ROPE_THETA = 10000.0
ROPE_DIM = 64
ROPE_PAD = 128
MOBA_BLOCK = 256
MOBA_TOPK = 3
MOE_TOPK = 2
MOE_ROW_BLOCK = 128
LANES = 128
VMEM_LIMIT = 56 * 1024 * 1024
NT_DIMS = (((1,), (1,)), ((), ()))


def _params(*sem):
    return pltpu.CompilerParams(dimension_semantics=sem, vmem_limit_bytes=VMEM_LIMIT)


def _tile(m, cands=(512, 256, 128, 64, 32, 16, 8)):
    for c in cands:
        if m % c == 0:
            return c
    raise ValueError(f"no tile for {m}")


def _rms(x, g):
    return x * lax.rsqrt(jnp.mean(x * x, axis=-1, keepdims=True) + NORM_EPS) * g


def _rope128(v, cos, sin):
    rot = pltpu.roll(v, 96, 1) + pltpu.roll(v, 32, 1)
    return v * cos + rot * sin


def _rmsnorm_body(x_ref, g_ref, o_ref):
    o_ref[...] = _rms(x_ref[...].astype(F32), g_ref[...]).astype(o_ref.dtype)


def _rmsnorm(x, g, out_dtype):
    m, d = x.shape
    tm = _tile(m)
    return pl.pallas_call(
        _rmsnorm_body, grid=(m // tm,),
        in_specs=[pl.BlockSpec((tm, d), lambda i: (i, 0)), pl.BlockSpec((1, d), lambda i: (0, 0))],
        out_specs=pl.BlockSpec((tm, d), lambda i: (i, 0)),
        out_shape=jax.ShapeDtypeStruct((m, d), out_dtype),
        compiler_params=_params("parallel"), name="rmsnorm")(x, g.reshape(1, d).astype(F32))


def _mm(a, b, fn, extras, extra_specs, out_shapes, out_specs, tm, tn, name):
    m, k = a.shape
    n = b.shape[1]
    n_ex = len(extras)

    def body(a_ref, b_ref, *refs):
        acc = jnp.dot(a_ref[...], b_ref[...], preferred_element_type=F32)
        res = fn(acc, *[r[...] for r in refs[:n_ex]])
        for o_ref, r in zip(refs[n_ex:], res):
            o_ref[...] = r.astype(o_ref.dtype)

    return pl.pallas_call(
        body, grid=(m // tm, n // tn),
        in_specs=[pl.BlockSpec((tm, k), lambda i, j: (i, 0)), pl.BlockSpec((k, tn), lambda i, j: (0, j))]
        + list(extra_specs),
        out_specs=out_specs, out_shape=out_shapes,
        compiler_params=_params("parallel", "arbitrary"), name=name)(a, b, *extras)


def _ckr_fn(acc, g, cos, sin):
    ckv = _rms(acc[:, :512], g)
    kr = _rope128(acc[:, 512:640], cos, sin)
    rows = jnp.concatenate([ckv, kr], axis=1)
    return rows[:, :576], rows


def _q_body(cq_ref, wn_ref, wr_ref, wuk_ref, cos_ref, sin_ref, q_ref):
    cq = cq_ref[...]
    qn = jnp.dot(cq, wn_ref[...], preferred_element_type=F32)
    ql = jnp.dot(qn.astype(BF16), wuk_ref[...], preferred_element_type=F32)
    qr = jnp.dot(cq, wr_ref[...], preferred_element_type=F32)
    qr = _rope128(qr, cos_ref[...], sin_ref[...])
    q_ref[:, :512] = ql.astype(BF16)
    q_ref[:, 512:] = qr.astype(BF16)


def _q_proj(cq, wn, wr, wuk, cos, sin):
    t, c = cq.shape
    h = wn.shape[0]
    tm = _tile(t)
    return pl.pallas_call(
        _q_body, grid=(t // tm, h),
        in_specs=[pl.BlockSpec((tm, c), lambda i, j: (i, 0)),
                  pl.BlockSpec((None, c, 128), lambda i, j: (j, 0, 0)),
                  pl.BlockSpec((None, c, ROPE_PAD), lambda i, j: (j, 0, 0)),
                  pl.BlockSpec((None, 128, 512), lambda i, j: (j, 0, 0)),
                  pl.BlockSpec((tm, ROPE_PAD), lambda i, j: (i, 0)),
                  pl.BlockSpec((tm, ROPE_PAD), lambda i, j: (i, 0))],
        out_specs=pl.BlockSpec((None, tm, 640), lambda i, j: (j, i, 0)),
        out_shape=jax.ShapeDtypeStruct((h, t, 640), BF16),
        compiler_params=_params("parallel", "arbitrary"), name="q_proj")(cq, wn, wr, wuk, cos, sin)


def _softmax_step(s, v, m_scr, l_scr, acc_scr):
    m_prev = m_scr[...]
    m_new = jnp.maximum(m_prev, jnp.max(s, axis=-1, keepdims=True))
    alpha = jnp.exp(m_prev - m_new)
    p = jnp.exp(s - m_new)
    l_scr[...] = alpha * l_scr[...] + jnp.sum(p, axis=-1, keepdims=True)
    acc_scr[...] = alpha * acc_scr[...] + jnp.dot(p.astype(BF16), v, preferred_element_type=F32)
    m_scr[...] = m_new


def _mla_prompt_body(q_ref, k_ref, wuv_ref, o_ref, m_scr, l_scr, acc_scr, *, tq, tk, heads, scale):
    i = pl.program_id(1)
    j = pl.program_id(2)
    last = (i * tq + tq - 1) // tk

    @pl.when(j == 0)
    def _():
        m_scr[...] = jnp.full_like(m_scr, NEG)
        l_scr[...] = jnp.zeros_like(l_scr)
        acc_scr[...] = jnp.zeros_like(acc_scr)

    def scores():
        q = q_ref[...].reshape(heads * tq, 640)
        k = k_ref[...]
        return lax.dot_general(q, k, NT_DIMS, preferred_element_type=F32) * scale, k[:, :512]

    @pl.when(j < last)
    def _():
        s, v = scores()
        _softmax_step(s, v, m_scr, l_scr, acc_scr)

    @pl.when(j == last)
    def _():
        s, v = scores()
        row = lax.broadcasted_iota(I32, s.shape, 0)
        col = lax.broadcasted_iota(I32, s.shape, 1)
        qpos = i * tq + row % tq
        s = jnp.where(j * tk + col <= qpos, s, NEG)
        _softmax_step(s, v, m_scr, l_scr, acc_scr)

    @pl.when(j == pl.num_programs(2) - 1)
    def _():
        lat = (acc_scr[...] / l_scr[...]).astype(BF16)
        for h in range(heads):
            o_ref[:, h * 128:(h + 1) * 128] = jnp.dot(
                lat[h * tq:(h + 1) * tq, :], wuv_ref[h], preferred_element_type=F32).astype(o_ref.dtype)


def _mla_prompt(q, kp, wuv, n_seq, seq, scale):
    heads = q.shape[0]
    tq = _tile(seq, (128,))
    tk = _tile(seq, (512, 256, 128))
    nq, nk = seq // tq, seq // tk
    body = functools.partial(_mla_prompt_body, tq=tq, tk=tk, heads=heads, scale=scale)
    return pl.pallas_call(
        body, grid=(n_seq, nq, nk),
        in_specs=[pl.BlockSpec((heads, tq, 640), lambda n, i, j: (0, n * nq + i, 0)),
                  pl.BlockSpec((tk, 640), lambda n, i, j: (n * nk + jnp.minimum(j, (i * tq + tq - 1) // tk), 0)),
                  pl.BlockSpec((heads, 512, 128), lambda n, i, j: (0, 0, 0))],
        out_specs=pl.BlockSpec((tq, heads * 128), lambda n, i, j: (n * nq + i, 0)),
        out_shape=jax.ShapeDtypeStruct((n_seq * seq, heads * 128), BF16),
        scratch_shapes=[pltpu.VMEM((heads * tq, 1), F32), pltpu.VMEM((heads * tq, 1), F32),
                        pltpu.VMEM((heads * tq, 512), F32)],
        compiler_params=_params("parallel", "parallel", "arbitrary"), name="mla_prompt")(q, kp, wuv)


def _mla_sample_body(pt_ref, q_ref, knew_ref, *refs, n_par, n_chunk, page, n_new, scale):
    page_refs = refs[:n_par]
    o_ref, kbuf, m_scr, l_scr, acc_scr = refs[n_par:]
    s_idx = pl.program_id(1)
    width = page_refs[0].shape[0]

    @pl.when(s_idx == 0)
    def _():
        m_scr[...] = jnp.full_like(m_scr, NEG)
        l_scr[...] = jnp.zeros_like(l_scr)
        acc_scr[...] = jnp.zeros_like(acc_scr)
        kbuf[width:, :] = jnp.zeros((kbuf.shape[0] - width, kbuf.shape[1]), BF16)

    for k in range(n_par):
        kbuf[:width, k * page:(k + 1) * page] = page_refs[k][...].astype(BF16)
    q = q_ref[...]
    cw = n_par * page // n_chunk
    parts = []
    for c in range(n_chunk):
        kc = kbuf[:, c * cw:(c + 1) * cw]
        s = jnp.dot(q, kc, preferred_element_type=F32) * scale
        mc = jnp.max(s, axis=-1, keepdims=True)
        p = jnp.exp(s - mc)
        oc = lax.dot_general(p.astype(BF16), kc[:512, :], NT_DIMS, preferred_element_type=F32)
        parts.append((mc, jnp.sum(p, axis=-1, keepdims=True), oc))
    m_prev = m_scr[...]
    m_new = m_prev
    for mc, _, _ in parts:
        m_new = jnp.maximum(m_new, mc)
    alpha = jnp.exp(m_prev - m_new)
    l_new = alpha * l_scr[...]
    acc = alpha * acc_scr[...]
    for mc, lc, oc in parts:
        w = jnp.exp(mc - m_new)
        l_new = l_new + w * lc
        acc = acc + w * oc
    m_scr[...] = m_new
    l_scr[...] = l_new
    acc_scr[...] = acc

    @pl.when(s_idx == pl.num_programs(1) - 1)
    def _():
        kn = knew_ref[...]
        sn = lax.dot_general(q, kn, NT_DIMS, preferred_element_type=F32) * scale
        row = lax.broadcasted_iota(I32, sn.shape, 0)
        col = lax.broadcasted_iota(I32, sn.shape, 1)
        sn = jnp.where(col <= row % n_new, sn, NEG)
        _softmax_step(sn, kn[:, :512], m_scr, l_scr, acc_scr)
        o_ref[...] = (acc_scr[...] / l_scr[...]).astype(o_ref.dtype)


def _mla_sample(page_table, q, knew, cache_t, layer, n_new, scale):
    b, rows, _ = q.shape
    n_pages = page_table.shape[1]
    width, page = cache_t.shape[2], cache_t.shape[3]
    n_par = _tile(n_pages, (32, 16, 8, 4, 2, 1))
    n_chunk = _tile(n_par, (4, 2, 1))
    assert n_new <= LANES and width % 16 == 0 and width <= 640
    body = functools.partial(_mla_sample_body, n_par=n_par, n_chunk=n_chunk, page=page, n_new=n_new, scale=scale)
    page_specs = [
        pl.BlockSpec((None, None, width, page),
                     functools.partial(lambda bi, si, pt, k: (layer, pt[bi, si * n_par + k], 0, 0), k=k))
        for k in range(n_par)]
    grid_spec = pltpu.PrefetchScalarGridSpec(
        num_scalar_prefetch=1, grid=(b, n_pages // n_par),
        in_specs=[pl.BlockSpec((None, rows, 640), lambda bi, si, pt: (bi, 0, 0)),
                  pl.BlockSpec((None, LANES, 640), lambda bi, si, pt: (bi, 0, 0))] + page_specs,
        out_specs=pl.BlockSpec((None, rows, 512), lambda bi, si, pt: (bi, 0, 0)),
        scratch_shapes=[pltpu.VMEM((640, n_par * page), BF16), pltpu.VMEM((rows, 1), F32),
                        pltpu.VMEM((rows, 1), F32), pltpu.VMEM((rows, 512), F32)])
    return pl.pallas_call(
        body, grid_spec=grid_spec, out_shape=jax.ShapeDtypeStruct((b, rows, 512), BF16),
        compiler_params=_params("parallel", "arbitrary"), name="mla_sample")(
            page_table, q, knew, *([cache_t] * n_par))


def _uv_body(lat_ref, w_ref, o_ref):
    o_ref[...] = jnp.dot(lat_ref[...], w_ref[...], preferred_element_type=F32).astype(o_ref.dtype)


def _uv_proj(lat, wuv):
    h, t, c = lat.shape
    return pl.pallas_call(
        _uv_body, grid=(h,),
        in_specs=[pl.BlockSpec((None, t, c), lambda j: (j, 0, 0)), pl.BlockSpec((None, c, 128), lambda j: (j, 0, 0))],
        out_specs=pl.BlockSpec((t, 128), lambda j: (0, j)),
        out_shape=jax.ShapeDtypeStruct((t, h * 128), BF16),
        compiler_params=_params("parallel"), name="uv_proj")(lat, wuv)


def _top_mask(gate, valid, topk):
    lane = lax.broadcasted_iota(I32, gate.shape, 1)
    g = jnp.where(valid, gate, NEG)
    sel = jnp.zeros(gate.shape, jnp.bool_)
    for _ in range(topk):
        mx = jnp.max(g, axis=-1, keepdims=True)
        idx = jnp.min(jnp.where(g == mx, lane, gate.shape[1]), axis=-1, keepdims=True)
        hit = lane == idx
        sel = sel | (hit & (mx > 0.5 * NEG))
        g = jnp.where(hit, -jnp.inf, g)
    return sel


def _kmean_body(k_ref, o_ref, *, nb):
    k = k_ref[...]
    w = k.shape[1]
    km = jnp.sum(k.reshape(nb, MOBA_BLOCK, w), axis=1) * (1.0 / MOBA_BLOCK)
    o_ref[...] = jnp.zeros_like(o_ref)
    o_ref[:nb, :] = km


def _kmean_prompt(kv, n_seq, seq, kw):
    nb = seq // MOBA_BLOCK
    assert nb <= LANES
    return pl.pallas_call(
        functools.partial(_kmean_body, nb=nb), grid=(n_seq,),
        in_specs=[pl.BlockSpec((seq, kw), lambda n: (n, 0))],
        out_specs=pl.BlockSpec((LANES, kw), lambda n: (n, 0)),
        out_shape=jax.ShapeDtypeStruct((n_seq * LANES, kw), F32),
        compiler_params=_params("parallel"), name="moba_kmean")(kv)


def _moba_prompt_body(q_ref, k_ref, v_ref, km_ref, o_ref, m_scr, l_scr, acc_scr, *, group, kv_heads, nb, scale):
    g = pl.program_id(1)
    i = pl.program_id(2)
    blk = MOBA_BLOCK
    rows = group * blk
    qv = q_ref[...]
    q = jnp.concatenate([qv[:, h * 128:(h + 1) * 128] for h in range(group)], axis=0)
    gate = lax.dot_general(q, km_ref[...].astype(BF16), NT_DIMS, preferred_element_type=F32)
    lane = lax.broadcasted_iota(I32, gate.shape, 1)
    sel = _top_mask(gate, lane < i, MOBA_TOPK)
    self = jnp.where(sel, 1.0, 0.0)

    row1 = lax.broadcasted_iota(I32, (rows, 1), 0)
    head = (g * group + row1 // blk + 1).astype(F32)
    slope = jnp.exp2(-8.0 * head / (group * kv_heads))
    row = lax.broadcasted_iota(I32, (rows, blk), 0)
    col = lax.broadcasted_iota(I32, (rows, blk), 1)
    rel = row % blk - col

    kd = k_ref[pl.ds(pl.multiple_of(i * blk, blk), blk), :]
    vd = v_ref[pl.ds(pl.multiple_of(i * blk, blk), blk), :]
    s = lax.dot_general(q, kd, NT_DIMS, preferred_element_type=F32) * scale - slope * rel.astype(F32)
    s = jnp.where(rel >= 0, s, NEG)
    m0 = jnp.max(s, axis=-1, keepdims=True)
    p = jnp.exp(s - m0)
    m_scr[...] = m0
    l_scr[...] = jnp.sum(p, axis=-1, keepdims=True)
    acc_scr[...] = jnp.dot(p.astype(BF16), vd, preferred_element_type=F32)

    for j in range(nb - 1):
        @pl.when(j < i)
        def _(j=j):
            kj = k_ref[j * blk:(j + 1) * blk, :]
            vj = v_ref[j * blk:(j + 1) * blk, :]
            dist = ((i - j) * blk + rel).astype(F32)
            sj = lax.dot_general(q, kj, NT_DIMS, preferred_element_type=F32) * scale - slope * dist
            sj = jnp.where(self[:, j:j + 1] > 0.5, sj, NEG)
            _softmax_step(sj, vj, m_scr, l_scr, acc_scr)

    o = acc_scr[...] / l_scr[...]
    for h in range(group):
        o_ref[:, h * 128:(h + 1) * 128] = o[h * blk:(h + 1) * blk, :].astype(o_ref.dtype)


def _moba_prompt(q, kvb, km, n_seq, seq, kv_heads, scale):
    heads = q.shape[1] // 128
    group = heads // kv_heads
    nb = seq // MOBA_BLOCK
    rows = group * MOBA_BLOCK
    body = functools.partial(_moba_prompt_body, group=group, kv_heads=kv_heads, nb=nb, scale=scale)
    return pl.pallas_call(
        body, grid=(n_seq, kv_heads, nb),
        in_specs=[pl.BlockSpec((MOBA_BLOCK, group * 128), lambda n, g, i: (n * nb + i, g)),
                  pl.BlockSpec((seq, 128), lambda n, g, i: (n, g)),
                  pl.BlockSpec((seq, 128), lambda n, g, i: (n, kv_heads + g)),
                  pl.BlockSpec((LANES, 128), lambda n, g, i: (n, g))],
        out_specs=pl.BlockSpec((MOBA_BLOCK, group * 128), lambda n, g, i: (n * nb + i, g)),
        out_shape=jax.ShapeDtypeStruct((n_seq * seq, q.shape[1]), BF16),
        scratch_shapes=[pltpu.VMEM((rows, 1), F32), pltpu.VMEM((rows, 1), F32), pltpu.VMEM((rows, 128), F32)],
        compiler_params=_params("parallel", "parallel", "arbitrary"), name="moba_prompt")(q, kvb, kvb, km)


def _moba_sample_body(pt_ref, q_ref, new_ref, *refs, n_par, page, kv_heads, group, n_new, past, scale):
    page_refs = refs[:n_par]
    o_ref, gate_scr, m_scr, l_scr, o_scr = refs[n_par:]
    s_idx = pl.program_id(1)
    ppb = MOBA_BLOCK // page
    bps = n_par // ppb
    nbp = past // MOBA_BLOCK
    rows = group * n_new
    lane = lax.broadcasted_iota(I32, (rows, LANES), 1)
    row1 = lax.broadcasted_iota(I32, (rows, 1), 0)
    t_q = row1 % n_new
    col = lax.broadcasted_iota(I32, (rows, MOBA_BLOCK), 1)

    def slope_of(g):
        head = (g * group + row1 // n_new + 1).astype(F32)
        return jnp.exp2(-8.0 * head / (group * kv_heads))

    @pl.when(s_idx == 0)
    def _():
        gate_scr[...] = jnp.full_like(gate_scr, NEG)
        m_scr[...] = jnp.full_like(m_scr, NEG)
        l_scr[...] = jnp.zeros_like(l_scr)

    n_which = 2 * kv_heads

    def rows_of(u, which):
        return page_refs[u][pl.ds(which, page, stride=n_which), :].astype(BF16)

    for g in range(kv_heads):
        q = q_ref[g]
        slope = slope_of(g)
        bias = slope * col.astype(F32)
        k_all = jnp.concatenate([rows_of(u, g) for u in range(n_par)], axis=0)
        s_raw = lax.dot_general(q, k_all, NT_DIMS, preferred_element_type=F32)
        gate_new, m_new, l_new = gate_scr[g], m_scr[g], l_scr[g]
        for bb in range(bps):
            blk = s_idx * bps + bb
            sb_raw = s_raw[:, bb * MOBA_BLOCK:(bb + 1) * MOBA_BLOCK]
            gate = jnp.sum(sb_raw, axis=-1, keepdims=True) * (1.0 / MOBA_BLOCK)
            sb = sb_raw * scale + bias
            mb = jnp.max(sb, axis=-1, keepdims=True)
            p = jnp.exp(sb - mb)
            v_b = jnp.concatenate([rows_of(bb * ppb + u, kv_heads + g) for u in range(ppb)], axis=0)
            o_scr[g, blk] = jnp.dot(p.astype(BF16), v_b, preferred_element_type=F32)
            row_c = slope * (past + t_q - blk * MOBA_BLOCK).astype(F32)
            hit = lane == blk
            gate_new = jnp.where(hit, gate, gate_new)
            m_new = jnp.where(hit, mb - row_c, m_new)
            l_new = jnp.where(hit, jnp.sum(p, axis=-1, keepdims=True), l_new)
        gate_scr[g] = gate_new
        m_scr[g] = m_new
        l_scr[g] = l_new

    @pl.when(s_idx == pl.num_programs(1) - 1)
    def _():
        for g in range(kv_heads):
            q = q_ref[g]
            kn = new_ref[g]
            vn = new_ref[kv_heads + g]
            coln = lax.broadcasted_iota(I32, (rows, LANES), 1)
            sn = lax.dot_general(q, kn, NT_DIMS, preferred_element_type=F32) * scale \
                - slope_of(g) * (t_q - coln).astype(F32)
            sn = jnp.where(coln <= t_q, sn, NEG)
            m_own = jnp.max(sn, axis=-1, keepdims=True)
            p_own = jnp.exp(sn - m_own)
            l_own = jnp.sum(p_own, axis=-1, keepdims=True)
            o_own = jnp.dot(p_own.astype(BF16), vn, preferred_element_type=F32)

            sel = _top_mask(gate_scr[g], lane < nbp, MOBA_TOPK)
            mb = jnp.where(sel, m_scr[g], NEG)
            m_all = jnp.maximum(m_own, jnp.max(mb, axis=-1, keepdims=True))
            w = jnp.where(sel, jnp.exp(mb - m_all), 0.0)
            w_own = jnp.exp(m_own - m_all)
            l_all = w_own * l_own + jnp.sum(w * l_scr[g], axis=-1, keepdims=True)
            o_all = w_own * o_own
            for j in range(nbp):
                o_all = o_all + w[:, j:j + 1] * o_scr[g, j]
            o_ref[g] = (o_all / l_all).astype(o_ref.dtype)


def _moba_sample(page_table, q, new, cache, layer, n_new, past, scale):
    b, kv_heads, rows, _ = q.shape
    group = rows // n_new
    n_pages = page_table.shape[1]
    page_rows = cache.shape[2]
    page = page_rows // (2 * kv_heads)
    nbp = past // MOBA_BLOCK
    n_par = _tile(n_pages, (32, 16, 8, 4, 2))
    assert MOBA_BLOCK % page == 0 and n_par % (MOBA_BLOCK // page) == 0 and cache.shape[3] == 128
    assert past % MOBA_BLOCK == 0 and nbp <= LANES and n_new <= MOBA_BLOCK and n_new <= LANES
    body = functools.partial(_moba_sample_body, n_par=n_par, page=page, kv_heads=kv_heads, group=group,
                             n_new=n_new, past=past, scale=scale)
    page_specs = [
        pl.BlockSpec((None, None, page_rows, 128),
                     functools.partial(lambda bi, si, pt, k: (layer, pt[bi, si * n_par + k], 0, 0), k=k))
        for k in range(n_par)]
    grid_spec = pltpu.PrefetchScalarGridSpec(
        num_scalar_prefetch=1, grid=(b, n_pages // n_par),
        in_specs=[pl.BlockSpec((None, kv_heads, rows, 128), lambda bi, si, pt: (bi, 0, 0, 0)),
                  pl.BlockSpec((None, 2 * kv_heads, LANES, 128), lambda bi, si, pt: (bi, 0, 0, 0))] + page_specs,
        out_specs=pl.BlockSpec((None, kv_heads, rows, 128), lambda bi, si, pt: (bi, 0, 0, 0)),
        scratch_shapes=[pltpu.VMEM((kv_heads, rows, LANES), F32), pltpu.VMEM((kv_heads, rows, LANES), F32),
                        pltpu.VMEM((kv_heads, rows, LANES), F32), pltpu.VMEM((kv_heads, nbp, rows, 128), F32)])
    return pl.pallas_call(
        body, grid_spec=grid_spec, out_shape=jax.ShapeDtypeStruct(q.shape, BF16),
        compiler_params=_params("parallel", "arbitrary"), name="moba_sample")(
            page_table, q, new, *([cache] * n_par))


def _merge_body(oa_ref, ob_ref, wa_ref, wb_ref, ga_ref, gb_ref, o_ref):
    bra = jnp.dot(oa_ref[...], wa_ref[...], preferred_element_type=F32)
    brb = jnp.dot(ob_ref[...], wb_ref[...], preferred_element_type=F32)
    o_ref[...] = (ga_ref[...].astype(F32) * bra + gb_ref[...].astype(F32) * brb).astype(o_ref.dtype)


def _merge(oa, ob, wa, wb, gates):
    t, ka = oa.shape
    kb = ob.shape[1]
    d = wa.shape[1]
    tm, tn = _tile(t), _tile(d)
    nj = d // tn
    return pl.pallas_call(
        _merge_body, grid=(t // tm, nj),
        in_specs=[pl.BlockSpec((tm, ka), lambda i, j: (i, 0)), pl.BlockSpec((tm, kb), lambda i, j: (i, 0)),
                  pl.BlockSpec((ka, tn), lambda i, j: (0, j)), pl.BlockSpec((kb, tn), lambda i, j: (0, j)),
                  pl.BlockSpec((tm, tn), lambda i, j: (i, j)), pl.BlockSpec((tm, tn), lambda i, j: (i, nj + j))],
        out_specs=pl.BlockSpec((tm, tn), lambda i, j: (i, j)),
        out_shape=jax.ShapeDtypeStruct((t, d), BF16),
        compiler_params=_params("parallel", "arbitrary"), name="merge")(oa, ob, wa, wb, gates, gates)


def _router_body(x_ref, g_ref, w_ref, b_ref, h_ref, id_ref, wt_ref, *, n_groups, per_group):
    h = _rms(x_ref[...], g_ref[...])
    h_ref[...] = h
    logits = jnp.dot(h, w_ref[...], preferred_element_type=F32, precision=lax.Precision.HIGHEST) + b_ref[...]
    lane = lax.broadcasted_iota(I32, logits.shape, 1)
    lg = jnp.where(lane < n_groups, logits, NEG)
    mg = jnp.max(lg, axis=-1, keepdims=True)
    g_sel = jnp.min(jnp.where(lg == mg, lane, LANES), axis=-1, keepdims=True)
    p_g = 1.0 / jnp.sum(jnp.exp(lg - mg), axis=-1, keepdims=True)
    e_lane = lane - n_groups
    in_group = (e_lane >= g_sel * per_group) & (e_lane < (g_sel + 1) * per_group)
    le = jnp.where(in_group, logits, NEG)
    m1 = jnp.max(le, axis=-1, keepdims=True)
    i1 = jnp.min(jnp.where(le == m1, lane, LANES), axis=-1, keepdims=True)
    le2 = jnp.where(lane == i1, NEG, le)
    m2 = jnp.max(le2, axis=-1, keepdims=True)
    i2 = jnp.min(jnp.where(le2 == m2, lane, LANES), axis=-1, keepdims=True)
    e2 = jnp.exp(m2 - m1)
    w1 = p_g / (1.0 + e2)
    w2 = p_g * e2 / (1.0 + e2)
    id_ref[...] = jnp.where(lane == 0, i1 - n_groups, jnp.where(lane == 1, i2 - n_groups, 0))
    wt_ref[...] = jnp.where(lane == 0, w1, jnp.where(lane == 1, w2, 0.0))


def _router(x, g, w, b, n_groups, per_group):
    t, d = x.shape
    tm = _tile(t, (256, 128, 64, 32, 16, 8))
    body = functools.partial(_router_body, n_groups=n_groups, per_group=per_group)
    return pl.pallas_call(
        body, grid=(t // tm,),
        in_specs=[pl.BlockSpec((tm, d), lambda i: (i, 0)), pl.BlockSpec((1, d), lambda i: (0, 0)),
                  pl.BlockSpec((d, LANES), lambda i: (0, 0)), pl.BlockSpec((1, LANES), lambda i: (0, 0))],
        out_specs=[pl.BlockSpec((tm, d), lambda i: (i, 0)), pl.BlockSpec((tm, LANES), lambda i: (i, 0)),
                   pl.BlockSpec((tm, LANES), lambda i: (i, 0))],
        out_shape=[jax.ShapeDtypeStruct((t, d), F32), jax.ShapeDtypeStruct((t, LANES), I32),
                   jax.ShapeDtypeStruct((t, LANES), F32)],
        compiler_params=_params("parallel"), name="ffn_norm_router")(x, g, w, b)


def _expert_body(src_ref, dst_ref, bexp_ref, nvalid_ref, h_hbm, wup_ref, wdn_ref, wp_ref, y_hbm,
                 xbuf, obuf, sem, *, rb, hidden):
    b = pl.program_id(0)

    def row_in(r):
        return pltpu.make_async_copy(h_hbm.at[pl.ds(src_ref[b * rb + r], 1), :], xbuf.at[pl.ds(r, 1), :], sem.at[0])

    def row_out(r):
        return pltpu.make_async_copy(obuf.at[pl.ds(r, 1), :], y_hbm.at[pl.ds(dst_ref[b * rb + r], 1), :], sem.at[1])

    @pl.when(b < nvalid_ref[0])
    def _():
        def start_in(r, c):
            row_in(r).start()
            return c
        lax.fori_loop(0, rb, start_in, 0)

        def wait_in(r, c):
            row_in(r).wait()
            return c
        lax.fori_loop(0, rb, wait_in, 0)

        x = xbuf[...].astype(BF16)
        gu = jnp.dot(x, wup_ref[...], preferred_element_type=F32)
        gate, up = gu[:, :hidden], gu[:, hidden:]
        act = gate * (1.0 / (1.0 + jnp.exp(-gate))) * up
        out = jnp.dot(act.astype(BF16), wdn_ref[...], preferred_element_type=F32)
        obuf[...] = out * wp_ref[...]

        def start_out(r, c):
            @pl.when(dst_ref[b * rb + r] >= 0)
            def _():
                row_out(r).start()
            return c
        lax.fori_loop(0, rb, start_out, 0)

        def wait_out(r, c):
            @pl.when(dst_ref[b * rb + r] >= 0)
            def _():
                row_out(r).wait()
            return c
        lax.fori_loop(0, rb, wait_out, 0)


def _experts(src, dst, bexp, nvalid, h, wup, wdn, wp, n_out_rows, rb):
    p = src.shape[0]
    n_blocks = p // rb
    d = h.shape[1]
    hidden = wdn.shape[1]
    grid_spec = pltpu.PrefetchScalarGridSpec(
        num_scalar_prefetch=4, grid=(n_blocks,),
        in_specs=[pl.BlockSpec(memory_space=pl.ANY),
                  pl.BlockSpec((None, d, 2 * hidden), lambda i, s, t, e, n: (e[i], 0, 0)),
                  pl.BlockSpec((None, hidden, d), lambda i, s, t, e, n: (e[i], 0, 0)),
                  pl.BlockSpec((rb, 1), lambda i, s, t, e, n: (i, 0))],
        out_specs=pl.BlockSpec(memory_space=pl.ANY),
        scratch_shapes=[pltpu.VMEM((rb, d), F32), pltpu.VMEM((rb, d), F32), pltpu.SemaphoreType.DMA((2,))])
    return pl.pallas_call(
        functools.partial(_expert_body, rb=rb, hidden=hidden), grid_spec=grid_spec,
        out_shape=jax.ShapeDtypeStruct((n_out_rows, d), F32),
        compiler_params=_params("arbitrary"), name="moe_experts")(src, dst, bexp, nvalid, h, wup, wdn, wp)


def _final_body(x_ref, y0_ref, y1_ref, g_ref, o_ref, *, norm):
    x = x_ref[...] + y0_ref[...] + y1_ref[...]
    o_ref[...] = _rms(x, g_ref[...]) if norm else x


def _final(x, y2, g, norm):
    t, d = x.shape
    tm = _tile(t, (256, 128, 64, 32, 16, 8))
    return pl.pallas_call(
        functools.partial(_final_body, norm=norm), grid=(t // tm,),
        in_specs=[pl.BlockSpec((tm, d), lambda i: (i, 0)), pl.BlockSpec((None, tm, d), lambda i: (0, i, 0)),
                  pl.BlockSpec((None, tm, d), lambda i: (1, i, 0)), pl.BlockSpec((1, d), lambda i: (0, 0))],
        out_specs=pl.BlockSpec((tm, d), lambda i: (i, 0)),
        out_shape=jax.ShapeDtypeStruct((t, d), F32),
        compiler_params=_params("parallel"), name="final_norm")(x, y2, y2, g)


def _dispatch(ids, wts, t, n_experts, rb):
    eid = ids.reshape(-1)
    a = eid.shape[0]
    order = jnp.argsort(eid, stable=True)
    eid_s = eid[order]
    counts = jnp.bincount(eid, length=n_experts)
    padded = (counts + rb - 1) // rb * rb
    pend = jnp.cumsum(padded)
    pstart = pend - padded
    cstart = jnp.cumsum(counts) - counts
    dest = pstart[eid_s] + (jnp.arange(a) - cstart[eid_s])
    n_blocks = (a + rb - 1) // rb + n_experts
    p = n_blocks * rb
    tok = (order // MOE_TOPK).astype(I32)
    slot = (order % MOE_TOPK).astype(I32)
    src = jnp.full((p,), t, I32).at[dest].set(tok)
    slot_p = jnp.zeros((p,), I32).at[dest].set(slot)
    wp = jnp.zeros((p,), F32).at[dest].set(wts.reshape(-1)[order])
    bexp = jnp.minimum(jnp.searchsorted(pend, jnp.arange(n_blocks) * rb, side='right'), n_experts - 1).astype(I32)
    nvalid = (pend[-1] // rb).astype(I32).reshape(1)
    return src, slot_p, wp, bexp, nvalid


def kernel(x_prompt, x_sample, cache_mla, cache_kv, page_table, g_attn_norm, w_in, g_q_norm, g_kv_norm, w_uq, w_uk,
           w_uv, w_branch_a, w_branch_b, w_out, g_ffn_norm, w_router_group, b_router_group, w_router_expert,
           b_router_expert, w_exp_up, w_exp_down, g_final):
    n_p, seq, d = x_prompt.shape
    n_s, n_new, _ = x_sample.shape
    depth = w_in.shape[0]
    n_pages, page = page_table.shape[1], cache_mla.shape[2]
    past = n_pages * page
    q_lora, heads_a = w_uq.shape[1], w_uq.shape[2]
    nope = w_uk.shape[3]
    kv_lora = w_uk.shape[1]
    lat_w = cache_mla.shape[3]
    kv_heads, hd = cache_kv.shape[4], cache_kv.shape[5]
    heads_b = w_branch_b.shape[1] // hd
    group = heads_b // kv_heads
    n_groups, per_group = b_router_expert.shape[1], b_router_expert.shape[2]
    n_experts = n_groups * per_group
    assert (kv_lora, nope, lat_w - kv_lora, hd, w_uv.shape[3]) == (512, 128, ROPE_DIM, 128, 128)
    assert n_groups + n_experts <= LANES
    tp, ts = n_p * seq, n_s * n_new
    t = tp + ts
    mla_scale = (nope + ROPE_DIM) ** -0.5
    moba_scale = hd ** -0.5
    kvw = kv_heads * hd

    half = ROPE_DIM // 2
    inv = ROPE_THETA ** (-jnp.arange(half, dtype=F32) / half)
    pos = jnp.concatenate([jnp.tile(jnp.arange(seq), n_p), jnp.tile(past + jnp.arange(n_new), n_s)])
    ang = pos.astype(F32)[:, None] * inv[None, :]
    zpad = jnp.zeros((t, ROPE_PAD - ROPE_DIM), F32)
    cos_t = jnp.concatenate([jnp.cos(ang), jnp.cos(ang), zpad], axis=1)
    sin_t = jnp.concatenate([-jnp.sin(ang), jnp.sin(ang), zpad], axis=1)

    x = jnp.concatenate([x_prompt.reshape(tp, d), x_sample.reshape(ts, d)], axis=0)
    new_mla_p, new_mla_s, new_kv_p, new_kv_s = [], [], [], []
    tm = _tile(t)
    for l in range(depth):
        o0 = q_lora
        o1 = o0 + lat_w
        o2 = o1 + heads_b * hd
        o3 = o2 + 2 * kvw
        wl = w_in[l]
        w_cq = wl[:, :o0].astype(BF16)
        w_ckr = jnp.pad(wl[:, o0:o1], ((0, 0), (0, 640 - lat_w))).astype(BF16)
        w_qb = wl[:, o1:o2].astype(BF16)
        w_kv = wl[:, o2:o3].astype(BF16)
        w_g = wl[:, o3:].astype(BF16)

        h = _rmsnorm(x, g_attn_norm[l], BF16)
        row_spec = lambda w: pl.BlockSpec((tm, w), lambda i, j: (i, 0))
        one_spec = lambda w: pl.BlockSpec((1, w), lambda i, j: (0, 0))
        (cq,) = _mm(h, w_cq, lambda acc, g: (_rms(acc, g),), [g_q_norm[l].reshape(1, -1)], [one_spec(q_lora)],
                    [jax.ShapeDtypeStruct((t, q_lora), BF16)], [pl.BlockSpec((tm, q_lora), lambda i, j: (i, 0))],
                    tm, q_lora, "in_proj_cq")
        mla_rows, kp = _mm(h, w_ckr, _ckr_fn, [g_kv_norm[l].reshape(1, -1), cos_t, sin_t],
                           [one_spec(kv_lora), row_spec(ROPE_PAD), row_spec(ROPE_PAD)],
                           [jax.ShapeDtypeStruct((t, lat_w), F32), jax.ShapeDtypeStruct((t, 640), BF16)],
                           [pl.BlockSpec((tm, lat_w), lambda i, j: (i, 0)), pl.BlockSpec((tm, 640), lambda i, j: (i, 0))],
                           tm, 640, "in_proj_ckr")
        tn_q = _tile(heads_b * hd, (1024, 512, 256, 128))
        (qb,) = _mm(h, w_qb, lambda acc: (acc,), [], [], [jax.ShapeDtypeStruct((t, heads_b * hd), BF16)],
                    [pl.BlockSpec((tm, tn_q), lambda i, j: (i, j))], tm, tn_q, "in_proj_qb")
        kv_rows, kvb = _mm(h, w_kv, lambda acc: (acc, acc), [], [],
                           [jax.ShapeDtypeStruct((t, 2 * kvw), F32), jax.ShapeDtypeStruct((t, 2 * kvw), BF16)],
                           [pl.BlockSpec((tm, 2 * kvw), lambda i, j: (i, 0))] * 2, tm, 2 * kvw, "in_proj_kv")
        tn_g = _tile(2 * d, (1024, 512, 256, 128))
        (gates,) = _mm(h, w_g, lambda acc: (1.0 / (1.0 + jnp.exp(-acc)),), [], [],
                       [jax.ShapeDtypeStruct((t, 2 * d), BF16)], [pl.BlockSpec((tm, tn_g), lambda i, j: (i, j))],
                       tm, tn_g, "in_proj_gates")

        wq = w_uq[l].transpose(1, 0, 2)
        wn = wq[:, :, :nope].astype(BF16)
        wr = jnp.pad(wq[:, :, nope:], ((0, 0), (0, 0), (0, ROPE_PAD - ROPE_DIM))).astype(BF16)
        wuk = w_uk[l].transpose(1, 2, 0).astype(BF16)
        wuv = w_uv[l].transpose(1, 0, 2).astype(BF16)
        qa = _q_proj(cq, wn, wr, wuk, cos_t, sin_t)

        oa_p = _mla_prompt(qa, kp, wuv, n_p, seq, mla_scale)
        km = _kmean_prompt(kv_rows, n_p, seq, kvw)
        ob_p = _moba_prompt(qb, kvb, km, n_p, seq, kv_heads, moba_scale)

        rows_a = heads_a * n_new
        qa_s = qa[:, tp:, :].reshape(heads_a, n_s, n_new, 640).transpose(1, 0, 2, 3).reshape(n_s, rows_a, 640)
        knew = jnp.pad(kp[tp:].reshape(n_s, n_new, 640), ((0, 0), (0, LANES - n_new), (0, 0)))
        lat_s = _mla_sample(page_table, qa_s, knew, jnp.swapaxes(cache_mla, 2, 3), l, n_new, mla_scale)
        lat_s = lat_s.reshape(n_s, heads_a, n_new, kv_lora).transpose(1, 0, 2, 3).reshape(heads_a, ts, kv_lora)
        oa_s = _uv_proj(lat_s, wuv)

        qb_s = qb[tp:].reshape(n_s, n_new, kv_heads, group, hd).transpose(0, 2, 3, 1, 4)
        qb_s = qb_s.reshape(n_s, kv_heads, group * n_new, hd)
        new_b = kvb[tp:].reshape(n_s, n_new, 2 * kv_heads, hd).transpose(0, 2, 1, 3)
        new_b = jnp.pad(new_b, ((0, 0), (0, 0), (0, LANES - n_new), (0, 0)))
        ckv = cache_kv.reshape(depth, cache_kv.shape[1], page * 2 * kv_heads, hd)
        ob_s = _moba_sample(page_table, qb_s, new_b, ckv, l, n_new, past, moba_scale)
        ob_s = ob_s.reshape(n_s, kv_heads, group, n_new, hd).transpose(0, 3, 1, 2, 4).reshape(ts, heads_b * hd)

        oa = jnp.concatenate([oa_p, oa_s], axis=0)
        ob = jnp.concatenate([ob_p, ob_s], axis=0)
        merged = _merge(oa, ob, w_branch_a[l].astype(BF16), w_branch_b[l].astype(BF16), gates)
        tn_o = _tile(d, (1024, 512, 256, 128))
        (x,) = _mm(merged, w_out[l].astype(BF16), lambda acc, xr: (acc + xr,), [x],
                   [pl.BlockSpec((tm, tn_o), lambda i, j: (i, j))], [jax.ShapeDtypeStruct((t, d), F32)],
                   [pl.BlockSpec((tm, tn_o), lambda i, j: (i, j))], tm, tn_o, "out_proj")

        new_mla_p.append(mla_rows[:tp].reshape(n_p, seq, lat_w))
        new_mla_s.append(mla_rows[tp:].reshape(n_s, n_new, lat_w))
        new_kv_p.append(kv_rows[:tp].reshape(n_p, seq, 2, kv_heads, hd))
        new_kv_s.append(kv_rows[tp:].reshape(n_s, n_new, 2, kv_heads, hd))

        w_r = jnp.concatenate([w_router_group[l], w_router_expert[l],
                               jnp.zeros((d, LANES - n_groups - n_experts), F32)], axis=1)
        b_r = jnp.concatenate([b_router_group[l], b_router_expert[l].reshape(-1),
                               jnp.zeros((LANES - n_groups - n_experts,), F32)]).reshape(1, LANES)
        h2, ids, wts = _router(x, g_ffn_norm[l].reshape(1, d), w_r, b_r, n_groups, per_group)
        rb = MOE_ROW_BLOCK
        src, slot, wp, bexp, nvalid = _dispatch(ids[:, :MOE_TOPK], wts[:, :MOE_TOPK], t, n_experts, rb)
        is_pad = src == t
        gsrc = jnp.where(is_pad, 0, src)
        dst = jnp.where(is_pad, -1, slot * t + src)
        y2 = _experts(gsrc, dst, bexp, nvalid, h2, w_exp_up[l].astype(BF16), w_exp_down[l].astype(BF16),
                      wp.reshape(-1, 1), MOE_TOPK * t, rb)
        y2 = y2.reshape(MOE_TOPK, t, d)
        last = l + 1 == depth
        x = _final(x, y2, g_final.reshape(1, d) if last else jnp.ones((1, d), F32), last)
    y = x
    return (y[:tp].reshape(n_p, seq, d), y[tp:].reshape(n_s, n_new, d),
            jnp.stack(new_mla_p), jnp.stack(new_mla_s), jnp.stack(new_kv_p), jnp.stack(new_kv_s))
```

```python
import functools

import jax
import jax.numpy as jnp
from jax import lax
from jax.experimental import pallas as pl
from jax.experimental.pallas import tpu as pltpu

F32 = jnp.float32
BF16 = jnp.bfloat16
I32 = jnp.int32

NORM_EPS = 1e-6
NEG = -1e30
ROPE_THETA = 10000.0
ROPE_DIM = 64
ROPE_PAD = 128
MOBA_BLOCK = 256
MOBA_TOPK = 3
MOE_TOPK = 2
MOE_ROW_BLOCK = 128
LANES = 128
VMEM_LIMIT = 56 * 1024 * 1024
NT_DIMS = (((1,), (1,)), ((), ()))


def _params(*sem):
    return pltpu.CompilerParams(dimension_semantics=sem, vmem_limit_bytes=VMEM_LIMIT)


def _tile(m, cands=(512, 256, 128, 64, 32, 16, 8)):
    for c in cands:
        if m % c == 0:
            return c
    raise ValueError(f"no tile for {m}")


def _rms(x, g):
    return x * lax.rsqrt(jnp.mean(x * x, axis=-1, keepdims=True) + NORM_EPS) * g


def _rope128(v, cos, sin):
    rot = pltpu.roll(v, 96, 1) + pltpu.roll(v, 32, 1)
    return v * cos + rot * sin


def _rmsnorm_body(x_ref, g_ref, o_ref):
    o_ref[...] = _rms(x_ref[...].astype(F32), g_ref[...]).astype(o_ref.dtype)


def _rmsnorm(x, g, out_dtype):
    m, d = x.shape
    tm = _tile(m)
    return pl.pallas_call(
        _rmsnorm_body, grid=(m // tm,),
        in_specs=[pl.BlockSpec((tm, d), lambda i: (i, 0)), pl.BlockSpec((1, d), lambda i: (0, 0))],
        out_specs=pl.BlockSpec((tm, d), lambda i: (i, 0)),
        out_shape=jax.ShapeDtypeStruct((m, d), out_dtype),
        compiler_params=_params("parallel"), name="rmsnorm")(x, g.reshape(1, d).astype(F32))


def _mm(a, b, fn, extras, extra_specs, out_shapes, out_specs, tm, tn, name):
    m, k = a.shape
    n = b.shape[1]
    n_ex = len(extras)

    def body(a_ref, b_ref, *refs):
        acc = jnp.dot(a_ref[...], b_ref[...], preferred_element_type=F32)
        res = fn(acc, *[r[...] for r in refs[:n_ex]])
        for o_ref, r in zip(refs[n_ex:], res):
            o_ref[...] = r.astype(o_ref.dtype)

    return pl.pallas_call(
        body, grid=(m // tm, n // tn),
        in_specs=[pl.BlockSpec((tm, k), lambda i, j: (i, 0)), pl.BlockSpec((k, tn), lambda i, j: (0, j))]
        + list(extra_specs),
        out_specs=out_specs, out_shape=out_shapes,
        compiler_params=_params("parallel", "arbitrary"), name=name)(a, b, *extras)


def _ckr_fn(acc, g, cos, sin):
    ckv = _rms(acc[:, :512], g)
    kr = _rope128(acc[:, 512:640], cos, sin)
    rows = jnp.concatenate([ckv, kr], axis=1)
    return rows[:, :576], rows


def _q_body(cq_ref, wn_ref, wr_ref, wuk_ref, cos_ref, sin_ref, q_ref):
    cq = cq_ref[...]
    qn = jnp.dot(cq, wn_ref[...], preferred_element_type=F32)
    ql = jnp.dot(qn.astype(BF16), wuk_ref[...], preferred_element_type=F32)
    qr = jnp.dot(cq, wr_ref[...], preferred_element_type=F32)
    qr = _rope128(qr, cos_ref[...], sin_ref[...])
    q_ref[:, :512] = ql.astype(BF16)
    q_ref[:, 512:] = qr.astype(BF16)


def _q_proj(cq, wn, wr, wuk, cos, sin):
    t, c = cq.shape
    h = wn.shape[0]
    tm = _tile(t)
    return pl.pallas_call(
        _q_body, grid=(t // tm, h),
        in_specs=[pl.BlockSpec((tm, c), lambda i, j: (i, 0)),
                  pl.BlockSpec((None, c, 128), lambda i, j: (j, 0, 0)),
                  pl.BlockSpec((None, c, ROPE_PAD), lambda i, j: (j, 0, 0)),
                  pl.BlockSpec((None, 128, 512), lambda i, j: (j, 0, 0)),
                  pl.BlockSpec((tm, ROPE_PAD), lambda i, j: (i, 0)),
                  pl.BlockSpec((tm, ROPE_PAD), lambda i, j: (i, 0))],
        out_specs=pl.BlockSpec((None, tm, 640), lambda i, j: (j, i, 0)),
        out_shape=jax.ShapeDtypeStruct((h, t, 640), BF16),
        compiler_params=_params("parallel", "arbitrary"), name="q_proj")(cq, wn, wr, wuk, cos, sin)


def _softmax_step(s, v, m_scr, l_scr, acc_scr):
    m_prev = m_scr[...]
    m_new = jnp.maximum(m_prev, jnp.max(s, axis=-1, keepdims=True))
    alpha = jnp.exp(m_prev - m_new)
    p = jnp.exp(s - m_new)
    l_scr[...] = alpha * l_scr[...] + jnp.sum(p, axis=-1, keepdims=True)
    acc_scr[...] = alpha * acc_scr[...] + jnp.dot(p.astype(BF16), v, preferred_element_type=F32)
    m_scr[...] = m_new


def _mla_prompt_body(q_ref, k_ref, wuv_ref, o_ref, m_scr, l_scr, acc_scr, *, tq, tk, heads, scale):
    i = pl.program_id(1)
    j = pl.program_id(2)
    last = (i * tq + tq - 1) // tk

    @pl.when(j == 0)
    def _():
        m_scr[...] = jnp.full_like(m_scr, NEG)
        l_scr[...] = jnp.zeros_like(l_scr)
        acc_scr[...] = jnp.zeros_like(acc_scr)

    def scores():
        q = q_ref[...].reshape(heads * tq, 640)
        k = k_ref[...]
        return lax.dot_general(q, k, NT_DIMS, preferred_element_type=F32) * scale, k[:, :512]

    @pl.when(j < last)
    def _():
        s, v = scores()
        _softmax_step(s, v, m_scr, l_scr, acc_scr)

    @pl.when(j == last)
    def _():
        s, v = scores()
        row = lax.broadcasted_iota(I32, s.shape, 0)
        col = lax.broadcasted_iota(I32, s.shape, 1)
        qpos = i * tq + row % tq
        s = jnp.where(j * tk + col <= qpos, s, NEG)
        _softmax_step(s, v, m_scr, l_scr, acc_scr)

    @pl.when(j == pl.num_programs(2) - 1)
    def _():
        lat = (acc_scr[...] / l_scr[...]).astype(BF16)
        for h in range(heads):
            o_ref[:, h * 128:(h + 1) * 128] = jnp.dot(
                lat[h * tq:(h + 1) * tq, :], wuv_ref[h], preferred_element_type=F32).astype(o_ref.dtype)


def _mla_prompt(q, kp, wuv, n_seq, seq, scale):
    heads = q.shape[0]
    tq = _tile(seq, (128,))
    tk = _tile(seq, (512, 256, 128))
    assert tk % tq == 0
    nq, nk = seq // tq, seq // tk
    body = functools.partial(_mla_prompt_body, tq=tq, tk=tk, heads=heads, scale=scale)
    return pl.pallas_call(
        body, grid=(n_seq, nq, nk),
        in_specs=[pl.BlockSpec((heads, tq, 640), lambda n, i, j: (0, n * nq + i, 0)),
                  pl.BlockSpec((tk, 640), lambda n, i, j: (n * nk + jnp.minimum(j, (i * tq + tq - 1) // tk), 0)),
                  pl.BlockSpec((heads, 512, 128), lambda n, i, j: (0, 0, 0))],
        out_specs=pl.BlockSpec((tq, heads * 128), lambda n, i, j: (n * nq + i, 0)),
        out_shape=jax.ShapeDtypeStruct((n_seq * seq, heads * 128), BF16),
        scratch_shapes=[pltpu.VMEM((heads * tq, 1), F32), pltpu.VMEM((heads * tq, 1), F32),
                        pltpu.VMEM((heads * tq, 512), F32)],
        compiler_params=_params("parallel", "parallel", "arbitrary"), name="mla_prompt")(q, kp, wuv)


def _fetch_pages(pt_ref, cache_hbm, pbuf, sem, layer, n_par):
    b, s = pl.program_id(0), pl.program_id(1)
    nb, ns = pl.num_programs(0), pl.num_programs(1)
    n = b * ns + s
    slot = n % 2

    def copies(bi, si, sl):
        return [pltpu.make_async_copy(cache_hbm.at[layer, pt_ref[bi, si * n_par + k]], pbuf.at[sl, k], sem.at[sl])
                for k in range(n_par)]

    @pl.when(n == 0)
    def _():
        for c in copies(b, s, slot):
            c.start()

    @pl.when(n + 1 < nb * ns)
    def _():
        wrap = s + 1 == ns
        for c in copies(jnp.where(wrap, b + 1, b), jnp.where(wrap, 0, s + 1), 1 - slot):
            c.start()

    for c in copies(b, s, slot):
        c.wait()
    return slot


def _mla_sample_body(pt_ref, q_ref, knew_ref, cache_hbm, o_ref, pbuf, sem, kbuf, m_scr, l_scr, acc_scr, *,
                     layer, n_par, n_chunk, n_new, scale):
    s_idx = pl.program_id(1)
    width, page = pbuf.shape[2], pbuf.shape[3]
    slot = _fetch_pages(pt_ref, cache_hbm, pbuf, sem, layer, n_par)

    @pl.when(s_idx == 0)
    def _():
        m_scr[...] = jnp.full_like(m_scr, NEG)
        l_scr[...] = jnp.zeros_like(l_scr)
        acc_scr[...] = jnp.zeros_like(acc_scr)
        kbuf[width:, :] = jnp.zeros((kbuf.shape[0] - width, kbuf.shape[1]), BF16)

    for k in range(n_par):
        kbuf[:width, k * page:(k + 1) * page] = pbuf[slot, k].astype(BF16)
    q = q_ref[...]
    cw = n_par * page // n_chunk
    parts = []
    for c in range(n_chunk):
        kc = kbuf[:, c * cw:(c + 1) * cw]
        s = jnp.dot(q, kc, preferred_element_type=F32) * scale
        mc = jnp.max(s, axis=-1, keepdims=True)
        p = jnp.exp(s - mc)
        oc = lax.dot_general(p.astype(BF16), kc[:512, :], NT_DIMS, preferred_element_type=F32)
        parts.append((mc, jnp.sum(p, axis=-1, keepdims=True), oc))
    m_prev = m_scr[...]
    m_new = m_prev
    for mc, _, _ in parts:
        m_new = jnp.maximum(m_new, mc)
    alpha = jnp.exp(m_prev - m_new)
    l_new = alpha * l_scr[...]
    acc = alpha * acc_scr[...]
    for mc, lc, oc in parts:
        w = jnp.exp(mc - m_new)
        l_new = l_new + w * lc
        acc = acc + w * oc
    m_scr[...] = m_new
    l_scr[...] = l_new
    acc_scr[...] = acc

    @pl.when(s_idx == pl.num_programs(1) - 1)
    def _():
        kn = knew_ref[...]
        sn = lax.dot_general(q, kn, NT_DIMS, preferred_element_type=F32) * scale
        row = lax.broadcasted_iota(I32, sn.shape, 0)
        col = lax.broadcasted_iota(I32, sn.shape, 1)
        sn = jnp.where(col <= row % n_new, sn, NEG)
        _softmax_step(sn, kn[:, :512], m_scr, l_scr, acc_scr)
        o_ref[...] = (acc_scr[...] / l_scr[...]).astype(o_ref.dtype)


def _mla_sample(page_table, q, knew, cache_t, layer, n_new, scale):
    b, rows, _ = q.shape
    n_pages = page_table.shape[1]
    width, page = cache_t.shape[2], cache_t.shape[3]
    n_par = _tile(n_pages, (32, 16, 8, 4, 2, 1))
    n_chunk = _tile(n_par, (4, 2, 1))
    assert n_new <= LANES and width % 16 == 0 and width <= 640
    body = functools.partial(_mla_sample_body, layer=layer, n_par=n_par, n_chunk=n_chunk, n_new=n_new, scale=scale)
    grid_spec = pltpu.PrefetchScalarGridSpec(
        num_scalar_prefetch=1, grid=(b, n_pages // n_par),
        in_specs=[pl.BlockSpec((None, rows, 640), lambda bi, si, pt: (bi, 0, 0)),
                  pl.BlockSpec((None, LANES, 640), lambda bi, si, pt: (bi, 0, 0)),
                  pl.BlockSpec(memory_space=pl.ANY)],
        out_specs=pl.BlockSpec((None, rows, 512), lambda bi, si, pt: (bi, 0, 0)),
        scratch_shapes=[pltpu.VMEM((2, n_par, width, page), F32), pltpu.SemaphoreType.DMA((2,)),
                        pltpu.VMEM((640, n_par * page), BF16), pltpu.VMEM((rows, 1), F32),
                        pltpu.VMEM((rows, 1), F32), pltpu.VMEM((rows, 512), F32)])
    return pl.pallas_call(
        body, grid_spec=grid_spec, out_shape=jax.ShapeDtypeStruct((b, rows, 512), BF16),
        compiler_params=_params("arbitrary", "arbitrary"), name="mla_sample")(page_table, q, knew, cache_t)


def _uv_body(lat_ref, w_ref, o_ref):
    o_ref[...] = jnp.dot(lat_ref[...], w_ref[...], preferred_element_type=F32).astype(o_ref.dtype)


def _uv_proj(lat, wuv):
    h, t, c = lat.shape
    return pl.pallas_call(
        _uv_body, grid=(h,),
        in_specs=[pl.BlockSpec((None, t, c), lambda j: (j, 0, 0)), pl.BlockSpec((None, c, 128), lambda j: (j, 0, 0))],
        out_specs=pl.BlockSpec((t, 128), lambda j: (0, j)),
        out_shape=jax.ShapeDtypeStruct((t, h * 128), BF16),
        compiler_params=_params("parallel"), name="uv_proj")(lat, wuv)


def _top_mask(gate, valid, topk):
    lane = lax.broadcasted_iota(I32, gate.shape, 1)
    g = jnp.where(valid, gate, NEG)
    sel = jnp.zeros(gate.shape, jnp.bool_)
    for _ in range(topk):
        mx = jnp.max(g, axis=-1, keepdims=True)
        idx = jnp.min(jnp.where(g == mx, lane, gate.shape[1]), axis=-1, keepdims=True)
        hit = lane == idx
        sel = sel | (hit & (mx > 0.5 * NEG))
        g = jnp.where(hit, -jnp.inf, g)
    return sel


def _kmean_body(k_ref, o_ref, *, nb):
    k = k_ref[...]
    w = k.shape[1]
    km = jnp.sum(k.reshape(nb, MOBA_BLOCK, w), axis=1) * (1.0 / MOBA_BLOCK)
    o_ref[...] = jnp.zeros_like(o_ref)
    o_ref[:nb, :] = km


def _kmean_prompt(kv, n_seq, seq, kw):
    nb = seq // MOBA_BLOCK
    assert nb <= LANES
    return pl.pallas_call(
        functools.partial(_kmean_body, nb=nb), grid=(n_seq,),
        in_specs=[pl.BlockSpec((seq, kw), lambda n: (n, 0))],
        out_specs=pl.BlockSpec((LANES, kw), lambda n: (n, 0)),
        out_shape=jax.ShapeDtypeStruct((n_seq * LANES, kw), F32),
        compiler_params=_params("parallel"), name="moba_kmean")(kv)


def _moba_prompt_body(q_ref, k_ref, v_ref, km_ref, o_ref, m_scr, l_scr, acc_scr, *, group, kv_heads, nb, scale):
    g = pl.program_id(1)
    i = pl.program_id(2)
    blk = MOBA_BLOCK
    rows = group * blk
    qv = q_ref[...]
    q = jnp.concatenate([qv[:, h * 128:(h + 1) * 128] for h in range(group)], axis=0)
    gate = lax.dot_general(q, km_ref[...].astype(BF16), NT_DIMS, preferred_element_type=F32)
    lane = lax.broadcasted_iota(I32, gate.shape, 1)
    sel = _top_mask(gate, lane < i, MOBA_TOPK)
    self = jnp.where(sel, 1.0, 0.0)

    row1 = lax.broadcasted_iota(I32, (rows, 1), 0)
    head = (g * group + row1 // blk + 1).astype(F32)
    slope = jnp.exp2(-8.0 * head / (group * kv_heads))
    row = lax.broadcasted_iota(I32, (rows, blk), 0)
    col = lax.broadcasted_iota(I32, (rows, blk), 1)
    rel = row % blk - col

    kd = k_ref[pl.ds(pl.multiple_of(i * blk, blk), blk), :]
    vd = v_ref[pl.ds(pl.multiple_of(i * blk, blk), blk), :]
    s = lax.dot_general(q, kd, NT_DIMS, preferred_element_type=F32) * scale - slope * rel.astype(F32)
    s = jnp.where(rel >= 0, s, NEG)
    m0 = jnp.max(s, axis=-1, keepdims=True)
    p = jnp.exp(s - m0)
    m_scr[...] = m0
    l_scr[...] = jnp.sum(p, axis=-1, keepdims=True)
    acc_scr[...] = jnp.dot(p.astype(BF16), vd, preferred_element_type=F32)

    for j in range(nb - 1):
        @pl.when(j < i)
        def _(j=j):
            kj = k_ref[j * blk:(j + 1) * blk, :]
            vj = v_ref[j * blk:(j + 1) * blk, :]
            dist = ((i - j) * blk + rel).astype(F32)
            sj = lax.dot_general(q, kj, NT_DIMS, preferred_element_type=F32) * scale - slope * dist
            sj = jnp.where(self[:, j:j + 1] > 0.5, sj, NEG)
            _softmax_step(sj, vj, m_scr, l_scr, acc_scr)

    o = acc_scr[...] / l_scr[...]
    for h in range(group):
        o_ref[:, h * 128:(h + 1) * 128] = o[h * blk:(h + 1) * blk, :].astype(o_ref.dtype)


def _moba_prompt(q, kvb, km, n_seq, seq, kv_heads, scale):
    heads = q.shape[1] // 128
    group = heads // kv_heads
    nb = seq // MOBA_BLOCK
    rows = group * MOBA_BLOCK
    body = functools.partial(_moba_prompt_body, group=group, kv_heads=kv_heads, nb=nb, scale=scale)
    return pl.pallas_call(
        body, grid=(n_seq, kv_heads, nb),
        in_specs=[pl.BlockSpec((MOBA_BLOCK, group * 128), lambda n, g, i: (n * nb + i, g)),
                  pl.BlockSpec((seq, 128), lambda n, g, i: (n, g)),
                  pl.BlockSpec((seq, 128), lambda n, g, i: (n, kv_heads + g)),
                  pl.BlockSpec((LANES, 128), lambda n, g, i: (n, g))],
        out_specs=pl.BlockSpec((MOBA_BLOCK, group * 128), lambda n, g, i: (n * nb + i, g)),
        out_shape=jax.ShapeDtypeStruct((n_seq * seq, q.shape[1]), BF16),
        scratch_shapes=[pltpu.VMEM((rows, 1), F32), pltpu.VMEM((rows, 1), F32), pltpu.VMEM((rows, 128), F32)],
        compiler_params=_params("parallel", "parallel", "arbitrary"), name="moba_prompt")(q, kvb, kvb, km)


def _moba_sample_body(pt_ref, q_ref, new_ref, cache_hbm, o_ref, pbuf, sem, gate_scr, m_scr, l_scr, o_scr, *,
                      layer, n_par, page, kv_heads, group, n_new, past, scale):
    s_idx = pl.program_id(1)
    slot = _fetch_pages(pt_ref, cache_hbm, pbuf, sem, layer, n_par)
    ppb = MOBA_BLOCK // page
    bps = n_par // ppb
    nbp = past // MOBA_BLOCK
    rows = group * n_new
    lane = lax.broadcasted_iota(I32, (rows, LANES), 1)
    row1 = lax.broadcasted_iota(I32, (rows, 1), 0)
    t_q = row1 % n_new
    col = lax.broadcasted_iota(I32, (rows, MOBA_BLOCK), 1)

    def slope_of(g):
        head = (g * group + row1 // n_new + 1).astype(F32)
        return jnp.exp2(-8.0 * head / (group * kv_heads))

    @pl.when(s_idx == 0)
    def _():
        gate_scr[...] = jnp.full_like(gate_scr, NEG)
        m_scr[...] = jnp.full_like(m_scr, NEG)
        l_scr[...] = jnp.zeros_like(l_scr)

    n_which = 2 * kv_heads

    def rows_of(u, which):
        return pbuf[slot, u, pl.ds(which, page, stride=n_which), :].astype(BF16)

    for g in range(kv_heads):
        q = q_ref[g]
        slope = slope_of(g)
        bias = slope * col.astype(F32)
        k_all = jnp.concatenate([rows_of(u, g) for u in range(n_par)], axis=0)
        s_raw = lax.dot_general(q, k_all, NT_DIMS, preferred_element_type=F32)
        gate_new, m_new, l_new = gate_scr[g], m_scr[g], l_scr[g]
        for bb in range(bps):
            blk = s_idx * bps + bb
            sb_raw = s_raw[:, bb * MOBA_BLOCK:(bb + 1) * MOBA_BLOCK]
            gate = jnp.sum(sb_raw, axis=-1, keepdims=True) * (1.0 / MOBA_BLOCK)
            sb = sb_raw * scale + bias
            mb = jnp.max(sb, axis=-1, keepdims=True)
            p = jnp.exp(sb - mb)
            v_b = jnp.concatenate([rows_of(bb * ppb + u, kv_heads + g) for u in range(ppb)], axis=0)
            o_scr[g, blk] = jnp.dot(p.astype(BF16), v_b, preferred_element_type=F32)
            row_c = slope * (past + t_q - blk * MOBA_BLOCK).astype(F32)
            hit = lane == blk
            gate_new = jnp.where(hit, gate, gate_new)
            m_new = jnp.where(hit, mb - row_c, m_new)
            l_new = jnp.where(hit, jnp.sum(p, axis=-1, keepdims=True), l_new)
        gate_scr[g] = gate_new
        m_scr[g] = m_new
        l_scr[g] = l_new

    @pl.when(s_idx == pl.num_programs(1) - 1)
    def _():
        for g in range(kv_heads):
            q = q_ref[g]
            kn = new_ref[g]
            vn = new_ref[kv_heads + g]
            coln = lax.broadcasted_iota(I32, (rows, LANES), 1)
            sn = lax.dot_general(q, kn, NT_DIMS, preferred_element_type=F32) * scale \
                - slope_of(g) * (t_q - coln).astype(F32)
            sn = jnp.where(coln <= t_q, sn, NEG)
            m_own = jnp.max(sn, axis=-1, keepdims=True)
            p_own = jnp.exp(sn - m_own)
            l_own = jnp.sum(p_own, axis=-1, keepdims=True)
            o_own = jnp.dot(p_own.astype(BF16), vn, preferred_element_type=F32)

            sel = _top_mask(gate_scr[g], lane < nbp, MOBA_TOPK)
            mb = jnp.where(sel, m_scr[g], NEG)
            m_all = jnp.maximum(m_own, jnp.max(mb, axis=-1, keepdims=True))
            w = jnp.where(sel, jnp.exp(mb - m_all), 0.0)
            w_own = jnp.exp(m_own - m_all)
            l_all = w_own * l_own + jnp.sum(w * l_scr[g], axis=-1, keepdims=True)
            o_all = w_own * o_own
            for j in range(nbp):
                o_all = o_all + w[:, j:j + 1] * o_scr[g, j]
            o_ref[g] = (o_all / l_all).astype(o_ref.dtype)


def _moba_sample(page_table, q, new, cache, layer, n_new, past, scale):
    b, kv_heads, rows, _ = q.shape
    group = rows // n_new
    n_pages = page_table.shape[1]
    page_rows = cache.shape[2]
    page = page_rows // (2 * kv_heads)
    nbp = past // MOBA_BLOCK
    n_par = _tile(n_pages, (32, 16, 8, 4, 2))
    assert MOBA_BLOCK % page == 0 and n_par % (MOBA_BLOCK // page) == 0 and cache.shape[3] == 128
    assert past % MOBA_BLOCK == 0 and nbp <= LANES and n_new <= MOBA_BLOCK and n_new <= LANES
    body = functools.partial(_moba_sample_body, layer=layer, n_par=n_par, page=page, kv_heads=kv_heads, group=group,
                             n_new=n_new, past=past, scale=scale)
    grid_spec = pltpu.PrefetchScalarGridSpec(
        num_scalar_prefetch=1, grid=(b, n_pages // n_par),
        in_specs=[pl.BlockSpec((None, kv_heads, rows, 128), lambda bi, si, pt: (bi, 0, 0, 0)),
                  pl.BlockSpec((None, 2 * kv_heads, LANES, 128), lambda bi, si, pt: (bi, 0, 0, 0)),
                  pl.BlockSpec(memory_space=pl.ANY)],
        out_specs=pl.BlockSpec((None, kv_heads, rows, 128), lambda bi, si, pt: (bi, 0, 0, 0)),
        scratch_shapes=[pltpu.VMEM((2, n_par, page_rows, 128), F32), pltpu.SemaphoreType.DMA((2,)),
                        pltpu.VMEM((kv_heads, rows, LANES), F32), pltpu.VMEM((kv_heads, rows, LANES), F32),
                        pltpu.VMEM((kv_heads, rows, LANES), F32), pltpu.VMEM((kv_heads, nbp, rows, 128), F32)])
    return pl.pallas_call(
        body, grid_spec=grid_spec, out_shape=jax.ShapeDtypeStruct(q.shape, BF16),
        compiler_params=_params("arbitrary", "arbitrary"), name="moba_sample")(page_table, q, new, cache)


def _merge_body(oa_ref, ob_ref, wa_ref, wb_ref, ga_ref, gb_ref, o_ref):
    bra = jnp.dot(oa_ref[...], wa_ref[...], preferred_element_type=F32)
    brb = jnp.dot(ob_ref[...], wb_ref[...], preferred_element_type=F32)
    o_ref[...] = (ga_ref[...].astype(F32) * bra + gb_ref[...].astype(F32) * brb).astype(o_ref.dtype)


def _merge(oa, ob, wa, wb, gates):
    t, ka = oa.shape
    kb = ob.shape[1]
    d = wa.shape[1]
    tm, tn = _tile(t), _tile(d)
    nj = d // tn
    return pl.pallas_call(
        _merge_body, grid=(t // tm, nj),
        in_specs=[pl.BlockSpec((tm, ka), lambda i, j: (i, 0)), pl.BlockSpec((tm, kb), lambda i, j: (i, 0)),
                  pl.BlockSpec((ka, tn), lambda i, j: (0, j)), pl.BlockSpec((kb, tn), lambda i, j: (0, j)),
                  pl.BlockSpec((tm, tn), lambda i, j: (i, j)), pl.BlockSpec((tm, tn), lambda i, j: (i, nj + j))],
        out_specs=pl.BlockSpec((tm, tn), lambda i, j: (i, j)),
        out_shape=jax.ShapeDtypeStruct((t, d), BF16),
        compiler_params=_params("parallel", "arbitrary"), name="merge")(oa, ob, wa, wb, gates, gates)


def _router_body(x_ref, g_ref, w_ref, b_ref, h_ref, id_ref, wt_ref, *, n_groups, per_group):
    h = _rms(x_ref[...], g_ref[...])
    h_ref[...] = h
    logits = jnp.dot(h, w_ref[...], preferred_element_type=F32, precision=lax.Precision.HIGHEST) + b_ref[...]
    lane = lax.broadcasted_iota(I32, logits.shape, 1)
    lg = jnp.where(lane < n_groups, logits, NEG)
    mg = jnp.max(lg, axis=-1, keepdims=True)
    g_sel = jnp.min(jnp.where(lg == mg, lane, LANES), axis=-1, keepdims=True)
    p_g = 1.0 / jnp.sum(jnp.exp(lg - mg), axis=-1, keepdims=True)
    e_lane = lane - n_groups
    in_group = (e_lane >= g_sel * per_group) & (e_lane < (g_sel + 1) * per_group)
    le = jnp.where(in_group, logits, NEG)
    m1 = jnp.max(le, axis=-1, keepdims=True)
    i1 = jnp.min(jnp.where(le == m1, lane, LANES), axis=-1, keepdims=True)
    le2 = jnp.where(lane == i1, NEG, le)
    m2 = jnp.max(le2, axis=-1, keepdims=True)
    i2 = jnp.min(jnp.where(le2 == m2, lane, LANES), axis=-1, keepdims=True)
    e2 = jnp.exp(m2 - m1)
    w1 = p_g / (1.0 + e2)
    w2 = p_g * e2 / (1.0 + e2)
    id_ref[...] = jnp.where(lane == 0, i1 - n_groups, jnp.where(lane == 1, i2 - n_groups, 0))
    wt_ref[...] = jnp.where(lane == 0, w1, jnp.where(lane == 1, w2, 0.0))


def _router(x, g, w, b, n_groups, per_group):
    t, d = x.shape
    tm = _tile(t, (256, 128, 64, 32, 16, 8))
    body = functools.partial(_router_body, n_groups=n_groups, per_group=per_group)
    return pl.pallas_call(
        body, grid=(t // tm,),
        in_specs=[pl.BlockSpec((tm, d), lambda i: (i, 0)), pl.BlockSpec((1, d), lambda i: (0, 0)),
                  pl.BlockSpec((d, LANES), lambda i: (0, 0)), pl.BlockSpec((1, LANES), lambda i: (0, 0))],
        out_specs=[pl.BlockSpec((tm, d), lambda i: (i, 0)), pl.BlockSpec((tm, LANES), lambda i: (i, 0)),
                   pl.BlockSpec((tm, LANES), lambda i: (i, 0))],
        out_shape=[jax.ShapeDtypeStruct((t, d), F32), jax.ShapeDtypeStruct((t, LANES), I32),
                   jax.ShapeDtypeStruct((t, LANES), F32)],
        compiler_params=_params("parallel"), name="ffn_norm_router")(x, g, w, b)


def _expert_body(src_ref, dst_ref, bexp_ref, cnt_ref, nvalid_ref, h_hbm, wup_ref, wdn_ref, wp_ref, y_hbm,
                 xbuf, obuf, sem_in, sem_out, *, rb, hidden):
    b = pl.program_id(0)
    nv = nvalid_ref[0]
    slot = b % 2

    def row_in(blk, sl, r):
        return pltpu.make_async_copy(h_hbm.at[pl.ds(src_ref[blk * rb + r], 1), :], xbuf.at[sl, pl.ds(r, 1), :],
                                     sem_in.at[sl])

    def row_out(blk, sl, r):
        return pltpu.make_async_copy(obuf.at[sl, pl.ds(r, 1), :], y_hbm.at[pl.ds(dst_ref[blk * rb + r], 1), :],
                                     sem_out.at[sl])

    def for_rows(blk, fn):
        def step(r, c):
            fn(r)
            return c
        lax.fori_loop(0, cnt_ref[blk], step, 0)

    @pl.when(b == 0)
    def _():
        xbuf[...] = jnp.zeros_like(xbuf)
        for_rows(0, lambda r: row_in(0, 0, r).start())

    @pl.when(b + 1 < nv)
    def _():
        for_rows(b + 1, lambda r: row_in(b + 1, 1 - slot, r).start())

    @pl.when(b < nv)
    def _():
        for_rows(b, lambda r: row_in(b, slot, r).wait())

        @pl.when(b >= 2)
        def _():
            for_rows(b - 2, lambda r: row_out(b - 2, slot, r).wait())

        x = xbuf[slot].astype(BF16)
        gu = jnp.dot(x, wup_ref[...], preferred_element_type=F32)
        gate, up = gu[:, :hidden], gu[:, hidden:]
        act = gate * (1.0 / (1.0 + jnp.exp(-gate))) * up
        out = jnp.dot(act.astype(BF16), wdn_ref[...], preferred_element_type=F32)
        obuf[slot] = out * wp_ref[...]
        for_rows(b, lambda r: row_out(b, slot, r).start())

        @pl.when(b == nv - 1)
        def _():
            @pl.when(b >= 1)
            def _():
                for_rows(b - 1, lambda r: row_out(b - 1, 1 - slot, r).wait())
            for_rows(b, lambda r: row_out(b, slot, r).wait())


def _experts(src, dst, bexp, cnt, nvalid, h, wup, wdn, wp, n_out_rows, rb):
    p = src.shape[0]
    n_blocks = p // rb
    d = h.shape[1]
    hidden = wdn.shape[1]
    grid_spec = pltpu.PrefetchScalarGridSpec(
        num_scalar_prefetch=5, grid=(n_blocks,),
        in_specs=[pl.BlockSpec(memory_space=pl.ANY),
                  pl.BlockSpec((None, d, 2 * hidden), lambda i, s, t, e, c, n: (e[i], 0, 0)),
                  pl.BlockSpec((None, hidden, d), lambda i, s, t, e, c, n: (e[i], 0, 0)),
                  pl.BlockSpec((rb, 1), lambda i, s, t, e, c, n: (i, 0))],
        out_specs=pl.BlockSpec(memory_space=pl.ANY),
        scratch_shapes=[pltpu.VMEM((2, rb, d), F32), pltpu.VMEM((2, rb, d), F32),
                        pltpu.SemaphoreType.DMA((2,)), pltpu.SemaphoreType.DMA((2,))])
    return pl.pallas_call(
        functools.partial(_expert_body, rb=rb, hidden=hidden), grid_spec=grid_spec,
        out_shape=jax.ShapeDtypeStruct((n_out_rows, d), F32),
        compiler_params=_params("arbitrary"), name="moe_experts")(src, dst, bexp, cnt, nvalid, h, wup, wdn, wp)


def _final_body(x_ref, y0_ref, y1_ref, g_ref, o_ref, *, norm):
    x = x_ref[...] + y0_ref[...] + y1_ref[...]
    o_ref[...] = _rms(x, g_ref[...]) if norm else x


def _final(x, y2, g, norm):
    t, d = x.shape
    tm = _tile(t, (256, 128, 64, 32, 16, 8))
    return pl.pallas_call(
        functools.partial(_final_body, norm=norm), grid=(t // tm,),
        in_specs=[pl.BlockSpec((tm, d), lambda i: (i, 0)), pl.BlockSpec((None, tm, d), lambda i: (0, i, 0)),
                  pl.BlockSpec((None, tm, d), lambda i: (1, i, 0)), pl.BlockSpec((1, d), lambda i: (0, 0))],
        out_specs=pl.BlockSpec((tm, d), lambda i: (i, 0)),
        out_shape=jax.ShapeDtypeStruct((t, d), F32),
        compiler_params=_params("parallel"), name="final_norm")(x, y2, y2, g)


def _dispatch(ids, wts, t, n_experts, rb):
    eid = ids.reshape(-1)
    a = eid.shape[0]
    order = jnp.argsort(eid, stable=True)
    eid_s = eid[order]
    counts = jnp.bincount(eid, length=n_experts)
    padded = (counts + rb - 1) // rb * rb
    pend = jnp.cumsum(padded)
    pstart = pend - padded
    cstart = jnp.cumsum(counts) - counts
    dest = pstart[eid_s] + (jnp.arange(a) - cstart[eid_s])
    n_blocks = (a + rb - 1) // rb + n_experts
    p = n_blocks * rb
    tok = (order // MOE_TOPK).astype(I32)
    slot = (order % MOE_TOPK).astype(I32)
    src = jnp.full((p,), t, I32).at[dest].set(tok)
    slot_p = jnp.zeros((p,), I32).at[dest].set(slot)
    wp = jnp.zeros((p,), F32).at[dest].set(wts.reshape(-1)[order])
    bexp = jnp.minimum(jnp.searchsorted(pend, jnp.arange(n_blocks) * rb, side='right'), n_experts - 1).astype(I32)
    nvalid = (pend[-1] // rb).astype(I32).reshape(1)
    cnt = jnp.sum((src != t).reshape(n_blocks, rb), axis=1).astype(I32)
    return src, slot_p, wp, bexp, cnt, nvalid


def kernel(x_prompt, x_sample, cache_mla, cache_kv, page_table, g_attn_norm, w_in, g_q_norm, g_kv_norm, w_uq, w_uk,
           w_uv, w_branch_a, w_branch_b, w_out, g_ffn_norm, w_router_group, b_router_group, w_router_expert,
           b_router_expert, w_exp_up, w_exp_down, g_final):
    n_p, seq, d = x_prompt.shape
    n_s, n_new, _ = x_sample.shape
    depth = w_in.shape[0]
    n_pages, page = page_table.shape[1], cache_mla.shape[2]
    past = n_pages * page
    q_lora, heads_a = w_uq.shape[1], w_uq.shape[2]
    nope = w_uk.shape[3]
    kv_lora = w_uk.shape[1]
    lat_w = cache_mla.shape[3]
    kv_heads, hd = cache_kv.shape[4], cache_kv.shape[5]
    heads_b = w_branch_b.shape[1] // hd
    group = heads_b // kv_heads
    n_groups, per_group = b_router_expert.shape[1], b_router_expert.shape[2]
    n_experts = n_groups * per_group
    assert (kv_lora, nope, lat_w - kv_lora, hd, w_uv.shape[3]) == (512, 128, ROPE_DIM, 128, 128)
    assert n_groups + n_experts <= LANES
    tp, ts = n_p * seq, n_s * n_new
    t = tp + ts
    mla_scale = (nope + ROPE_DIM) ** -0.5
    moba_scale = hd ** -0.5
    kvw = kv_heads * hd

    half = ROPE_DIM // 2
    inv = ROPE_THETA ** (-jnp.arange(half, dtype=F32) / half)
    pos = jnp.concatenate([jnp.tile(jnp.arange(seq), n_p), jnp.tile(past + jnp.arange(n_new), n_s)])
    ang = pos.astype(F32)[:, None] * inv[None, :]
    zpad = jnp.zeros((t, ROPE_PAD - ROPE_DIM), F32)
    cos_t = jnp.concatenate([jnp.cos(ang), jnp.cos(ang), zpad], axis=1)
    sin_t = jnp.concatenate([-jnp.sin(ang), jnp.sin(ang), zpad], axis=1)

    x = jnp.concatenate([x_prompt.reshape(tp, d), x_sample.reshape(ts, d)], axis=0)
    new_mla_p, new_mla_s, new_kv_p, new_kv_s = [], [], [], []
    tm = _tile(t)
    for l in range(depth):
        o0 = q_lora
        o1 = o0 + lat_w
        o2 = o1 + heads_b * hd
        o3 = o2 + 2 * kvw
        wl = w_in[l]
        w_cq = wl[:, :o0].astype(BF16)
        w_ckr = jnp.pad(wl[:, o0:o1], ((0, 0), (0, 640 - lat_w))).astype(BF16)
        w_qb = wl[:, o1:o2].astype(BF16)
        w_kv = wl[:, o2:o3].astype(BF16)
        w_g = wl[:, o3:].astype(BF16)

        h = _rmsnorm(x, g_attn_norm[l], BF16)
        row_spec = lambda w: pl.BlockSpec((tm, w), lambda i, j: (i, 0))
        one_spec = lambda w: pl.BlockSpec((1, w), lambda i, j: (0, 0))
        (cq,) = _mm(h, w_cq, lambda acc, g: (_rms(acc, g),), [g_q_norm[l].reshape(1, -1)], [one_spec(q_lora)],
                    [jax.ShapeDtypeStruct((t, q_lora), BF16)], [pl.BlockSpec((tm, q_lora), lambda i, j: (i, 0))],
                    tm, q_lora, "in_proj_cq")
        mla_rows, kp = _mm(h, w_ckr, _ckr_fn, [g_kv_norm[l].reshape(1, -1), cos_t, sin_t],
                           [one_spec(kv_lora), row_spec(ROPE_PAD), row_spec(ROPE_PAD)],
                           [jax.ShapeDtypeStruct((t, lat_w), F32), jax.ShapeDtypeStruct((t, 640), BF16)],
                           [pl.BlockSpec((tm, lat_w), lambda i, j: (i, 0)), pl.BlockSpec((tm, 640), lambda i, j: (i, 0))],
                           tm, 640, "in_proj_ckr")
        tn_q = _tile(heads_b * hd, (1024, 512, 256, 128))
        (qb,) = _mm(h, w_qb, lambda acc: (acc,), [], [], [jax.ShapeDtypeStruct((t, heads_b * hd), BF16)],
                    [pl.BlockSpec((tm, tn_q), lambda i, j: (i, j))], tm, tn_q, "in_proj_qb")
        kv_rows, kvb = _mm(h, w_kv, lambda acc: (acc, acc), [], [],
                           [jax.ShapeDtypeStruct((t, 2 * kvw), F32), jax.ShapeDtypeStruct((t, 2 * kvw), BF16)],
                           [pl.BlockSpec((tm, 2 * kvw), lambda i, j: (i, 0))] * 2, tm, 2 * kvw, "in_proj_kv")
        tn_g = _tile(2 * d, (1024, 512, 256, 128))
        (gates,) = _mm(h, w_g, lambda acc: (1.0 / (1.0 + jnp.exp(-acc)),), [], [],
                       [jax.ShapeDtypeStruct((t, 2 * d), BF16)], [pl.BlockSpec((tm, tn_g), lambda i, j: (i, j))],
                       tm, tn_g, "in_proj_gates")

        wq = w_uq[l].transpose(1, 0, 2)
        wn = wq[:, :, :nope].astype(BF16)
        wr = jnp.pad(wq[:, :, nope:], ((0, 0), (0, 0), (0, ROPE_PAD - ROPE_DIM))).astype(BF16)
        wuk = w_uk[l].transpose(1, 2, 0).astype(BF16)
        wuv = w_uv[l].transpose(1, 0, 2).astype(BF16)
        qa = _q_proj(cq, wn, wr, wuk, cos_t, sin_t)

        oa_p = _mla_prompt(qa, kp, wuv, n_p, seq, mla_scale)
        km = _kmean_prompt(kv_rows, n_p, seq, kvw)
        ob_p = _moba_prompt(qb, kvb, km, n_p, seq, kv_heads, moba_scale)

        rows_a = heads_a * n_new
        qa_s = qa[:, tp:, :].reshape(heads_a, n_s, n_new, 640).transpose(1, 0, 2, 3).reshape(n_s, rows_a, 640)
        knew = jnp.pad(kp[tp:].reshape(n_s, n_new, 640), ((0, 0), (0, LANES - n_new), (0, 0)))
        lat_s = _mla_sample(page_table, qa_s, knew, jnp.swapaxes(cache_mla, 2, 3), l, n_new, mla_scale)
        lat_s = lat_s.reshape(n_s, heads_a, n_new, kv_lora).transpose(1, 0, 2, 3).reshape(heads_a, ts, kv_lora)
        oa_s = _uv_proj(lat_s, wuv)

        qb_s = qb[tp:].reshape(n_s, n_new, kv_heads, group, hd).transpose(0, 2, 3, 1, 4)
        qb_s = qb_s.reshape(n_s, kv_heads, group * n_new, hd)
        new_b = kvb[tp:].reshape(n_s, n_new, 2 * kv_heads, hd).transpose(0, 2, 1, 3)
        new_b = jnp.pad(new_b, ((0, 0), (0, 0), (0, LANES - n_new), (0, 0)))
        ckv = cache_kv.reshape(depth, cache_kv.shape[1], page * 2 * kv_heads, hd)
        ob_s = _moba_sample(page_table, qb_s, new_b, ckv, l, n_new, past, moba_scale)
        ob_s = ob_s.reshape(n_s, kv_heads, group, n_new, hd).transpose(0, 3, 1, 2, 4).reshape(ts, heads_b * hd)

        oa = jnp.concatenate([oa_p, oa_s], axis=0)
        ob = jnp.concatenate([ob_p, ob_s], axis=0)
        merged = _merge(oa, ob, w_branch_a[l].astype(BF16), w_branch_b[l].astype(BF16), gates)
        tn_o = _tile(d, (1024, 512, 256, 128))
        (x,) = _mm(merged, w_out[l].astype(BF16), lambda acc, xr: (acc + xr,), [x],
                   [pl.BlockSpec((tm, tn_o), lambda i, j: (i, j))], [jax.ShapeDtypeStruct((t, d), F32)],
                   [pl.BlockSpec((tm, tn_o), lambda i, j: (i, j))], tm, tn_o, "out_proj")

        new_mla_p.append(mla_rows[:tp].reshape(n_p, seq, lat_w))
        new_mla_s.append(mla_rows[tp:].reshape(n_s, n_new, lat_w))
        new_kv_p.append(kv_rows[:tp].reshape(n_p, seq, 2, kv_heads, hd))
        new_kv_s.append(kv_rows[tp:].reshape(n_s, n_new, 2, kv_heads, hd))

        w_r = jnp.concatenate([w_router_group[l], w_router_expert[l],
                               jnp.zeros((d, LANES - n_groups - n_experts), F32)], axis=1)
        b_r = jnp.concatenate([b_router_group[l], b_router_expert[l].reshape(-1),
                               jnp.zeros((LANES - n_groups - n_experts,), F32)]).reshape(1, LANES)
        h2, ids, wts = _router(x, g_ffn_norm[l].reshape(1, d), w_r, b_r, n_groups, per_group)
        rb = MOE_ROW_BLOCK
        src, slot, wp, bexp, cnt, nvalid = _dispatch(ids[:, :MOE_TOPK], wts[:, :MOE_TOPK], t, n_experts, rb)
        is_pad = src == t
        gsrc = jnp.where(is_pad, 0, src)
        dst = jnp.where(is_pad, 0, slot * t + src)
        y2 = _experts(gsrc, dst, bexp, cnt, nvalid, h2, w_exp_up[l].astype(BF16), w_exp_down[l].astype(BF16),
                      wp.reshape(-1, 1), MOE_TOPK * t, rb)
        y2 = y2.reshape(MOE_TOPK, t, d)
        last = l + 1 == depth
        x = _final(x, y2, g_final.reshape(1, d) if last else jnp.ones((1, d), F32), last)
    y = x
    return (y[:tp].reshape(n_p, seq, d), y[tp:].reshape(n_s, n_new, d),
            jnp.stack(new_mla_p), jnp.stack(new_mla_s), jnp.stack(new_kv_p), jnp.stack(new_kv_s))
```

```python
import functools
import math

import jax
import jax.numpy as jnp
from jax import lax
from jax.experimental import pallas as pl
from jax.experimental.pallas import tpu as pltpu

F32 = jnp.float32
BF16 = jnp.bfloat16
I32 = jnp.int32

NORM_EPS = 1e-6
NEG = -1e30
ROPE_THETA = 10000.0
ROPE_DIM = 64
ROPE_PAD = 128
MOBA_BLOCK = 256
MOBA_TOPK = 3
MOE_TOPK = 2
MOE_ROW_BLOCK = 128
PAGE_SLOTS = 3
LOG2E = 1.4426950408889634
LANES = 128
VMEM_LIMIT = 56 * 1024 * 1024
NT_DIMS = (((1,), (1,)), ((), ()))


def _params(*sem):
    return pltpu.CompilerParams(dimension_semantics=sem, vmem_limit_bytes=VMEM_LIMIT)


def _tile(m, cands=(512, 256, 128, 64, 32, 16, 8)):
    for c in cands:
        if m % c == 0:
            return c
    raise ValueError(f"no tile for {m}")


def _rms(x, g):
    return x * lax.rsqrt(jnp.mean(x * x, axis=-1, keepdims=True) + NORM_EPS) * g


def _rope128(v, cos, sin):
    rot = pltpu.roll(v, 96, 1) + pltpu.roll(v, 32, 1)
    return v * cos + rot * sin


def _rmsnorm_body(x_ref, g_ref, o_ref):
    o_ref[...] = _rms(x_ref[...].astype(F32), g_ref[...]).astype(o_ref.dtype)


def _rmsnorm(x, g, out_dtype):
    m, d = x.shape
    tm = _tile(m)
    return pl.pallas_call(
        _rmsnorm_body, grid=(m // tm,),
        in_specs=[pl.BlockSpec((tm, d), lambda i: (i, 0)), pl.BlockSpec((1, d), lambda i: (0, 0))],
        out_specs=pl.BlockSpec((tm, d), lambda i: (i, 0)),
        out_shape=jax.ShapeDtypeStruct((m, d), out_dtype),
        compiler_params=_params("parallel"), name="rmsnorm")(x, g.reshape(1, d).astype(F32))


def _mm(a, b, fn, extras, extra_specs, out_shapes, out_specs, tm, tn, name):
    m, k = a.shape
    n = b.shape[1]
    n_ex = len(extras)

    def body(a_ref, b_ref, *refs):
        acc = jnp.dot(a_ref[...], b_ref[...], preferred_element_type=F32)
        res = fn(acc, *[r[...] for r in refs[:n_ex]])
        for o_ref, r in zip(refs[n_ex:], res):
            o_ref[...] = r.astype(o_ref.dtype)

    return pl.pallas_call(
        body, grid=(m // tm, n // tn),
        in_specs=[pl.BlockSpec((tm, k), lambda i, j: (i, 0)), pl.BlockSpec((k, tn), lambda i, j: (0, j))]
        + list(extra_specs),
        out_specs=out_specs, out_shape=out_shapes,
        compiler_params=_params("parallel", "arbitrary"), name=name)(a, b, *extras)


def _ckr_fn(acc, g, cos, sin):
    ckv = _rms(acc[:, :512], g)
    kr = _rope128(acc[:, 512:640], cos, sin)
    rows = jnp.concatenate([ckv, kr], axis=1)
    return rows[:, :576], rows


def _q_body(cq_ref, wn_ref, wr_ref, wuk_ref, cos_ref, sin_ref, q_ref, *, hb):
    cq = cq_ref[...]
    qn = jnp.dot(cq, wn_ref[...], preferred_element_type=F32).astype(BF16)
    qr = jnp.dot(cq, wr_ref[...], preferred_element_type=F32)
    cos, sin = cos_ref[...], sin_ref[...]
    for h in range(hb):
        lanes = slice(h * 128, (h + 1) * 128)
        q_ref[h, :, :512] = jnp.dot(qn[:, lanes], wuk_ref[h], preferred_element_type=F32).astype(BF16)
        q_ref[h, :, 512:] = _rope128(qr[:, lanes], cos, sin).astype(BF16)


def _q_proj(cq, wn, wr, wuk, cos, sin):
    t, c = cq.shape
    h = wuk.shape[0]
    hb = _tile(h, (4, 2, 1))
    tm = _tile(t)
    return pl.pallas_call(
        functools.partial(_q_body, hb=hb), grid=(t // tm, h // hb),
        in_specs=[pl.BlockSpec((tm, c), lambda i, j: (i, 0)),
                  pl.BlockSpec((c, hb * 128), lambda i, j: (0, j)),
                  pl.BlockSpec((c, hb * ROPE_PAD), lambda i, j: (0, j)),
                  pl.BlockSpec((hb, 128, 512), lambda i, j: (j, 0, 0)),
                  pl.BlockSpec((tm, ROPE_PAD), lambda i, j: (i, 0)),
                  pl.BlockSpec((tm, ROPE_PAD), lambda i, j: (i, 0))],
        out_specs=pl.BlockSpec((hb, tm, 640), lambda i, j: (j, i, 0)),
        out_shape=jax.ShapeDtypeStruct((h, t, 640), BF16),
        compiler_params=_params("parallel", "arbitrary"), name="q_proj")(cq, wn, wr, wuk, cos, sin)


def _softmax_step(s, v, m_scr, l_scr, acc_scr):
    m_prev = m_scr[...]
    m_new = jnp.maximum(m_prev, jnp.max(s, axis=-1, keepdims=True))
    alpha = jnp.exp2(m_prev - m_new)
    p = jnp.exp2(s - m_new)
    l_scr[...] = alpha * l_scr[...] + jnp.sum(p, axis=-1, keepdims=True)
    acc_scr[...] = alpha * acc_scr[...] + jnp.dot(p.astype(BF16), v, preferred_element_type=F32)
    m_scr[...] = m_new


def _mla_prompt_body(q_ref, k_ref, wuv_ref, o_ref, m_scr, l_scr, acc_scr, *, tq, tk, heads, scale):
    i = pl.program_id(1)
    j = pl.program_id(2)
    last = (i * tq + tq - 1) // tk

    @pl.when(j == 0)
    def _():
        m_scr[...] = jnp.full_like(m_scr, NEG)
        l_scr[...] = jnp.zeros_like(l_scr)
        acc_scr[...] = jnp.zeros_like(acc_scr)

    def scores():
        q = q_ref[...].reshape(heads * tq, 640)
        k = k_ref[...]
        return lax.dot_general(q, k, NT_DIMS, preferred_element_type=F32) * scale, k[:, :512]

    @pl.when(j < last)
    def _():
        s, v = scores()
        _softmax_step(s, v, m_scr, l_scr, acc_scr)

    @pl.when(j == last)
    def _():
        s, v = scores()
        row = lax.broadcasted_iota(I32, s.shape, 0)
        col = lax.broadcasted_iota(I32, s.shape, 1)
        qpos = i * tq + row % tq
        s = jnp.where(j * tk + col <= qpos, s, NEG)
        _softmax_step(s, v, m_scr, l_scr, acc_scr)

    @pl.when(j == pl.num_programs(2) - 1)
    def _():
        lat = (acc_scr[...] / l_scr[...]).astype(BF16)
        for h in range(heads):
            o_ref[:, h * 128:(h + 1) * 128] = jnp.dot(
                lat[h * tq:(h + 1) * tq, :], wuv_ref[h], preferred_element_type=F32).astype(o_ref.dtype)


def _mla_prompt(q, kp, wuv, n_seq, seq, scale):
    heads = q.shape[0]
    tq = _tile(seq, (128,))
    tk = _tile(seq, (512, 256, 128))
    assert tk % tq == 0
    nq, nk = seq // tq, seq // tk
    body = functools.partial(_mla_prompt_body, tq=tq, tk=tk, heads=heads, scale=scale * LOG2E)
    return pl.pallas_call(
        body, grid=(n_seq, nq, nk),
        in_specs=[pl.BlockSpec((heads, tq, 640), lambda n, i, j: (0, n * nq + i, 0)),
                  pl.BlockSpec((tk, 640), lambda n, i, j: (n * nk + jnp.minimum(j, (i * tq + tq - 1) // tk), 0)),
                  pl.BlockSpec((heads, 512, 128), lambda n, i, j: (0, 0, 0))],
        out_specs=pl.BlockSpec((tq, heads * 128), lambda n, i, j: (n * nq + i, 0)),
        out_shape=jax.ShapeDtypeStruct((n_seq * seq, heads * 128), BF16),
        scratch_shapes=[pltpu.VMEM((heads * tq, 1), F32), pltpu.VMEM((heads * tq, 1), F32),
                        pltpu.VMEM((heads * tq, 512), F32)],
        compiler_params=_params("parallel", "parallel", "arbitrary"), name="mla_prompt")(q, kp, wuv)


def _fetch_pages(pt_ref, cache_hbm, pbuf, sem, layer, n_par):
    n_slots = pbuf.shape[0]
    ahead = n_slots - 1
    ns = pl.num_programs(1)
    total = pl.num_programs(0) * ns
    n = pl.program_id(0) * ns + pl.program_id(1)

    def copies(step):
        bi, si, sl = step // ns, step % ns, step % n_slots
        return [pltpu.make_async_copy(cache_hbm.at[layer, pt_ref[bi, si * n_par + k]], pbuf.at[sl, k], sem.at[sl])
                for k in range(n_par)]

    @pl.when(n == 0)
    def _():
        for a in range(ahead):
            @pl.when(a < total)
            def _(a=a):
                for c in copies(a):
                    c.start()

    @pl.when(n + ahead < total)
    def _():
        for c in copies(n + ahead):
            c.start()

    for c in copies(n):
        c.wait()
    return n % n_slots


def _mla_sample_body(pt_ref, q_ref, knew_ref, cache_hbm, o_ref, pbuf, sem, kbuf, m_scr, l_scr, acc_scr, *,
                     layer, n_par, n_chunk, n_new, scale):
    s_idx = pl.program_id(1)
    width, page = pbuf.shape[2], pbuf.shape[3]
    slot = _fetch_pages(pt_ref, cache_hbm, pbuf, sem, layer, n_par)

    @pl.when(s_idx == 0)
    def _():
        m_scr[...] = jnp.full_like(m_scr, NEG)
        l_scr[...] = jnp.zeros_like(l_scr)
        acc_scr[...] = jnp.zeros_like(acc_scr)
        kbuf[width:, :] = jnp.zeros((kbuf.shape[0] - width, kbuf.shape[1]), BF16)

    for k in range(n_par):
        kbuf[:width, k * page:(k + 1) * page] = pbuf[slot, k].astype(BF16)
    q = q_ref[...]
    cw = n_par * page // n_chunk
    parts = []
    for c in range(n_chunk):
        kc = kbuf[:, c * cw:(c + 1) * cw]
        s = jnp.dot(q, kc, preferred_element_type=F32) * scale
        mc = jnp.max(s, axis=-1, keepdims=True)
        p = jnp.exp2(s - mc)
        oc = lax.dot_general(p.astype(BF16), kc[:512, :], NT_DIMS, preferred_element_type=F32)
        parts.append((mc, jnp.sum(p, axis=-1, keepdims=True), oc))
    m_prev = m_scr[...]
    m_new = m_prev
    for mc, _, _ in parts:
        m_new = jnp.maximum(m_new, mc)
    alpha = jnp.exp2(m_prev - m_new)
    l_new = alpha * l_scr[...]
    acc = alpha * acc_scr[...]
    for mc, lc, oc in parts:
        w = jnp.exp2(mc - m_new)
        l_new = l_new + w * lc
        acc = acc + w * oc
    m_scr[...] = m_new
    l_scr[...] = l_new
    acc_scr[...] = acc

    @pl.when(s_idx == pl.num_programs(1) - 1)
    def _():
        kn = knew_ref[...]
        sn = lax.dot_general(q, kn, NT_DIMS, preferred_element_type=F32) * scale
        row = lax.broadcasted_iota(I32, sn.shape, 0)
        col = lax.broadcasted_iota(I32, sn.shape, 1)
        sn = jnp.where(col <= row % n_new, sn, NEG)
        _softmax_step(sn, kn[:, :512], m_scr, l_scr, acc_scr)
        o_ref[...] = (acc_scr[...] / l_scr[...]).astype(o_ref.dtype)


def _mla_sample(page_table, q, knew, cache_t, layer, n_new, scale):
    b, rows, _ = q.shape
    n_pages = page_table.shape[1]
    width, page = cache_t.shape[2], cache_t.shape[3]
    n_par = _tile(n_pages, (32, 16, 8, 4, 2, 1))
    n_chunk = _tile(n_par, (4, 2, 1))
    assert n_new <= LANES and width % 16 == 0 and width <= 640
    body = functools.partial(_mla_sample_body, layer=layer, n_par=n_par, n_chunk=n_chunk, n_new=n_new,
                             scale=scale * LOG2E)
    grid_spec = pltpu.PrefetchScalarGridSpec(
        num_scalar_prefetch=1, grid=(b, n_pages // n_par),
        in_specs=[pl.BlockSpec((None, rows, 640), lambda bi, si, pt: (bi, 0, 0)),
                  pl.BlockSpec((None, LANES, 640), lambda bi, si, pt: (bi, 0, 0)),
                  pl.BlockSpec(memory_space=pl.ANY)],
        out_specs=pl.BlockSpec((None, rows, 512), lambda bi, si, pt: (bi, 0, 0)),
        scratch_shapes=[pltpu.VMEM((PAGE_SLOTS, n_par, width, page), F32), pltpu.SemaphoreType.DMA((PAGE_SLOTS,)),
                        pltpu.VMEM((640, n_par * page), BF16), pltpu.VMEM((rows, 1), F32),
                        pltpu.VMEM((rows, 1), F32), pltpu.VMEM((rows, 512), F32)])
    return pl.pallas_call(
        body, grid_spec=grid_spec, out_shape=jax.ShapeDtypeStruct((b, rows, 512), BF16),
        compiler_params=_params("arbitrary", "arbitrary"), name="mla_sample")(page_table, q, knew, cache_t)


def _uv_body(lat_ref, w_ref, o_ref):
    o_ref[...] = jnp.dot(lat_ref[...], w_ref[...], preferred_element_type=F32).astype(o_ref.dtype)


def _uv_proj(lat, wuv):
    h, t, c = lat.shape
    return pl.pallas_call(
        _uv_body, grid=(h,),
        in_specs=[pl.BlockSpec((None, t, c), lambda j: (j, 0, 0)), pl.BlockSpec((None, c, 128), lambda j: (j, 0, 0))],
        out_specs=pl.BlockSpec((t, 128), lambda j: (0, j)),
        out_shape=jax.ShapeDtypeStruct((t, h * 128), BF16),
        compiler_params=_params("parallel"), name="uv_proj")(lat, wuv)


def _top_mask(gate, valid, topk):
    lane = lax.broadcasted_iota(I32, gate.shape, 1)
    g = jnp.where(valid, gate, NEG)
    sel = jnp.zeros(gate.shape, jnp.bool_)
    for _ in range(topk):
        mx = jnp.max(g, axis=-1, keepdims=True)
        idx = jnp.min(jnp.where(g == mx, lane, gate.shape[1]), axis=-1, keepdims=True)
        hit = lane == idx
        sel = sel | (hit & (mx > 0.5 * NEG))
        g = jnp.where(hit, -jnp.inf, g)
    return sel


def _kmean_body(k_ref, o_ref, *, nb):
    k = k_ref[...]
    w = k.shape[1]
    km = jnp.sum(k.reshape(nb, MOBA_BLOCK, w), axis=1) * (1.0 / MOBA_BLOCK)
    o_ref[...] = jnp.zeros_like(o_ref)
    o_ref[:nb, :] = km


def _kmean_prompt(kv, n_seq, seq, kw):
    nb = seq // MOBA_BLOCK
    assert nb <= LANES
    return pl.pallas_call(
        functools.partial(_kmean_body, nb=nb), grid=(n_seq,),
        in_specs=[pl.BlockSpec((seq, kw), lambda n: (n, 0))],
        out_specs=pl.BlockSpec((LANES, kw), lambda n: (n, 0)),
        out_shape=jax.ShapeDtypeStruct((n_seq * LANES, kw), F32),
        compiler_params=_params("parallel"), name="moba_kmean")(kv)


def _moba_prompt_body(q_ref, k_ref, v_ref, km_ref, o_ref, m_scr, l_scr, acc_scr, *, group, kv_heads, nb, scale):
    g = pl.program_id(1)
    i = pl.program_id(2)
    blk = MOBA_BLOCK
    rows = group * blk
    qv = q_ref[...]
    q = jnp.concatenate([qv[:, h * 128:(h + 1) * 128] for h in range(group)], axis=0)
    gate = lax.dot_general(q, km_ref[...].astype(BF16), NT_DIMS, preferred_element_type=F32)
    lane = lax.broadcasted_iota(I32, gate.shape, 1)
    sel = _top_mask(gate, lane < i, MOBA_TOPK)
    self = jnp.where(sel, 1.0, 0.0)

    row1 = lax.broadcasted_iota(I32, (rows, 1), 0)
    head = (g * group + row1 // blk + 1).astype(F32)
    slope = LOG2E * jnp.exp2(-8.0 * head / (group * kv_heads))
    row = lax.broadcasted_iota(I32, (rows, blk), 0)
    col = lax.broadcasted_iota(I32, (rows, blk), 1)
    rel = row % blk - col

    kd = k_ref[pl.ds(pl.multiple_of(i * blk, blk), blk), :]
    vd = v_ref[pl.ds(pl.multiple_of(i * blk, blk), blk), :]
    s = lax.dot_general(q, kd, NT_DIMS, preferred_element_type=F32) * scale - slope * rel.astype(F32)
    s = jnp.where(rel >= 0, s, NEG)
    m0 = jnp.max(s, axis=-1, keepdims=True)
    p = jnp.exp2(s - m0)
    m_scr[...] = m0
    l_scr[...] = jnp.sum(p, axis=-1, keepdims=True)
    acc_scr[...] = jnp.dot(p.astype(BF16), vd, preferred_element_type=F32)

    for j in range(nb - 1):
        @pl.when(j < i)
        def _(j=j):
            kj = k_ref[j * blk:(j + 1) * blk, :]
            vj = v_ref[j * blk:(j + 1) * blk, :]
            dist = ((i - j) * blk + rel).astype(F32)
            sj = lax.dot_general(q, kj, NT_DIMS, preferred_element_type=F32) * scale - slope * dist
            sj = jnp.where(self[:, j:j + 1] > 0.5, sj, NEG)
            _softmax_step(sj, vj, m_scr, l_scr, acc_scr)

    o = acc_scr[...] / l_scr[...]
    for h in range(group):
        o_ref[:, h * 128:(h + 1) * 128] = o[h * blk:(h + 1) * blk, :].astype(o_ref.dtype)


def _moba_prompt(q, kvb, km, n_seq, seq, kv_heads, scale):
    heads = q.shape[1] // 128
    group = heads // kv_heads
    nb = seq // MOBA_BLOCK
    rows = group * MOBA_BLOCK
    body = functools.partial(_moba_prompt_body, group=group, kv_heads=kv_heads, nb=nb, scale=scale * LOG2E)
    return pl.pallas_call(
        body, grid=(n_seq, kv_heads, nb),
        in_specs=[pl.BlockSpec((MOBA_BLOCK, group * 128), lambda n, g, i: (n * nb + i, g)),
                  pl.BlockSpec((seq, 128), lambda n, g, i: (n, g)),
                  pl.BlockSpec((seq, 128), lambda n, g, i: (n, kv_heads + g)),
                  pl.BlockSpec((LANES, 128), lambda n, g, i: (n, g))],
        out_specs=pl.BlockSpec((MOBA_BLOCK, group * 128), lambda n, g, i: (n * nb + i, g)),
        out_shape=jax.ShapeDtypeStruct((n_seq * seq, q.shape[1]), BF16),
        scratch_shapes=[pltpu.VMEM((rows, 1), F32), pltpu.VMEM((rows, 1), F32), pltpu.VMEM((rows, 128), F32)],
        compiler_params=_params("parallel", "parallel", "arbitrary"), name="moba_prompt")(q, kvb, kvb, km)


def _moba_sample_body(pt_ref, q_ref, new_ref, cache_hbm, o_ref, pbuf, sem, gate_scr, m_scr, l_scr, o_scr, *,
                      layer, n_par, page, kv_heads, group, n_new, past, scale):
    s_idx = pl.program_id(1)
    slot = _fetch_pages(pt_ref, cache_hbm, pbuf, sem, layer, n_par)
    ppb = MOBA_BLOCK // page
    bps = n_par // ppb
    nbp = past // MOBA_BLOCK
    rows = group * n_new
    lane = lax.broadcasted_iota(I32, (rows, LANES), 1)
    row1 = lax.broadcasted_iota(I32, (rows, 1), 0)
    t_q = row1 % n_new
    col = lax.broadcasted_iota(I32, (rows, MOBA_BLOCK), 1)

    def slope_of(g):
        head = (g * group + row1 // n_new + 1).astype(F32)
        return LOG2E * jnp.exp2(-8.0 * head / (group * kv_heads))

    @pl.when(s_idx == 0)
    def _():
        gate_scr[...] = jnp.full_like(gate_scr, NEG)
        m_scr[...] = jnp.full_like(m_scr, NEG)
        l_scr[...] = jnp.zeros_like(l_scr)

    n_which = 2 * kv_heads

    def rows_of(u, which):
        return pbuf[slot, u, pl.ds(which, page, stride=n_which), :].astype(BF16)

    for g in range(kv_heads):
        q = q_ref[g]
        slope = slope_of(g)
        bias = slope * col.astype(F32)
        k_all = jnp.concatenate([rows_of(u, g) for u in range(n_par)], axis=0)
        s_raw = lax.dot_general(q, k_all, NT_DIMS, preferred_element_type=F32)
        gate_new, m_new, l_new = gate_scr[g], m_scr[g], l_scr[g]
        for bb in range(bps):
            blk = s_idx * bps + bb
            sb_raw = s_raw[:, bb * MOBA_BLOCK:(bb + 1) * MOBA_BLOCK]
            gate = jnp.sum(sb_raw, axis=-1, keepdims=True) * (1.0 / MOBA_BLOCK)
            sb = sb_raw * scale + bias
            mb = jnp.max(sb, axis=-1, keepdims=True)
            p = jnp.exp2(sb - mb)
            v_b = jnp.concatenate([rows_of(bb * ppb + u, kv_heads + g) for u in range(ppb)], axis=0)
            o_scr[g, blk] = jnp.dot(p.astype(BF16), v_b, preferred_element_type=F32)
            row_c = slope * (past + t_q - blk * MOBA_BLOCK).astype(F32)
            hit = lane == blk
            gate_new = jnp.where(hit, gate, gate_new)
            m_new = jnp.where(hit, mb - row_c, m_new)
            l_new = jnp.where(hit, jnp.sum(p, axis=-1, keepdims=True), l_new)
        gate_scr[g] = gate_new
        m_scr[g] = m_new
        l_scr[g] = l_new

    @pl.when(s_idx == pl.num_programs(1) - 1)
    def _():
        for g in range(kv_heads):
            q = q_ref[g]
            kn = new_ref[g]
            vn = new_ref[kv_heads + g]
            coln = lax.broadcasted_iota(I32, (rows, LANES), 1)
            sn = lax.dot_general(q, kn, NT_DIMS, preferred_element_type=F32) * scale \
                - slope_of(g) * (t_q - coln).astype(F32)
            sn = jnp.where(coln <= t_q, sn, NEG)
            m_own = jnp.max(sn, axis=-1, keepdims=True)
            p_own = jnp.exp2(sn - m_own)
            l_own = jnp.sum(p_own, axis=-1, keepdims=True)
            o_own = jnp.dot(p_own.astype(BF16), vn, preferred_element_type=F32)

            sel = _top_mask(gate_scr[g], lane < nbp, MOBA_TOPK)
            mb = jnp.where(sel, m_scr[g], NEG)
            m_all = jnp.maximum(m_own, jnp.max(mb, axis=-1, keepdims=True))
            w = jnp.where(sel, jnp.exp2(mb - m_all), 0.0)
            w_own = jnp.exp2(m_own - m_all)
            l_all = w_own * l_own + jnp.sum(w * l_scr[g], axis=-1, keepdims=True)
            o_all = w_own * o_own
            for j in range(nbp):
                o_all = o_all + w[:, j:j + 1] * o_scr[g, j]
            o_ref[g] = (o_all / l_all).astype(o_ref.dtype)


def _moba_sample(page_table, q, new, cache, layer, n_new, past, scale):
    b, kv_heads, rows, _ = q.shape
    group = rows // n_new
    n_pages = page_table.shape[1]
    page_rows = cache.shape[2]
    page = page_rows // (2 * kv_heads)
    nbp = past // MOBA_BLOCK
    n_par = _tile(n_pages, (32, 16, 8, 4, 2))
    assert MOBA_BLOCK % page == 0 and n_par % (MOBA_BLOCK // page) == 0 and cache.shape[3] == 128
    assert past % MOBA_BLOCK == 0 and nbp <= LANES and n_new <= MOBA_BLOCK and n_new <= LANES
    body = functools.partial(_moba_sample_body, layer=layer, n_par=n_par, page=page, kv_heads=kv_heads, group=group,
                             n_new=n_new, past=past, scale=scale * LOG2E)
    grid_spec = pltpu.PrefetchScalarGridSpec(
        num_scalar_prefetch=1, grid=(b, n_pages // n_par),
        in_specs=[pl.BlockSpec((None, kv_heads, rows, 128), lambda bi, si, pt: (bi, 0, 0, 0)),
                  pl.BlockSpec((None, 2 * kv_heads, LANES, 128), lambda bi, si, pt: (bi, 0, 0, 0)),
                  pl.BlockSpec(memory_space=pl.ANY)],
        out_specs=pl.BlockSpec((None, kv_heads, rows, 128), lambda bi, si, pt: (bi, 0, 0, 0)),
        scratch_shapes=[pltpu.VMEM((PAGE_SLOTS, n_par, page_rows, 128), F32), pltpu.SemaphoreType.DMA((PAGE_SLOTS,)),
                        pltpu.VMEM((kv_heads, rows, LANES), F32), pltpu.VMEM((kv_heads, rows, LANES), F32),
                        pltpu.VMEM((kv_heads, rows, LANES), F32), pltpu.VMEM((kv_heads, nbp, rows, 128), F32)])
    return pl.pallas_call(
        body, grid_spec=grid_spec, out_shape=jax.ShapeDtypeStruct(q.shape, BF16),
        compiler_params=_params("arbitrary", "arbitrary"), name="moba_sample")(page_table, q, new, cache)


def _merge_body(oa_ref, ob_ref, wa_ref, wb_ref, ga_ref, gb_ref, o_ref):
    bra = jnp.dot(oa_ref[...], wa_ref[...], preferred_element_type=F32)
    brb = jnp.dot(ob_ref[...], wb_ref[...], preferred_element_type=F32)
    o_ref[...] = (ga_ref[...].astype(F32) * bra + gb_ref[...].astype(F32) * brb).astype(o_ref.dtype)


def _merge(oa, ob, wa, wb, gates):
    t, ka = oa.shape
    kb = ob.shape[1]
    d = wa.shape[1]
    tm, tn = _tile(t), _tile(d)
    nj = d // tn
    return pl.pallas_call(
        _merge_body, grid=(t // tm, nj),
        in_specs=[pl.BlockSpec((tm, ka), lambda i, j: (i, 0)), pl.BlockSpec((tm, kb), lambda i, j: (i, 0)),
                  pl.BlockSpec((ka, tn), lambda i, j: (0, j)), pl.BlockSpec((kb, tn), lambda i, j: (0, j)),
                  pl.BlockSpec((tm, tn), lambda i, j: (i, j)), pl.BlockSpec((tm, tn), lambda i, j: (i, nj + j))],
        out_specs=pl.BlockSpec((tm, tn), lambda i, j: (i, j)),
        out_shape=jax.ShapeDtypeStruct((t, d), BF16),
        compiler_params=_params("parallel", "arbitrary"), name="merge")(oa, ob, wa, wb, gates, gates)


def _router_body(x_ref, g_ref, w_ref, b_ref, h_ref, id_ref, wt_ref, *, n_groups, per_group):
    h = _rms(x_ref[...], g_ref[...])
    h_ref[...] = h
    logits = jnp.dot(h, w_ref[...], preferred_element_type=F32, precision=lax.Precision.HIGHEST) + b_ref[...]
    lane = lax.broadcasted_iota(I32, logits.shape, 1)
    lg = jnp.where(lane < n_groups, logits, NEG)
    mg = jnp.max(lg, axis=-1, keepdims=True)
    g_sel = jnp.min(jnp.where(lg == mg, lane, LANES), axis=-1, keepdims=True)
    p_g = 1.0 / jnp.sum(jnp.exp(lg - mg), axis=-1, keepdims=True)
    e_lane = lane - n_groups
    in_group = (e_lane >= g_sel * per_group) & (e_lane < (g_sel + 1) * per_group)
    le = jnp.where(in_group, logits, NEG)
    m1 = jnp.max(le, axis=-1, keepdims=True)
    i1 = jnp.min(jnp.where(le == m1, lane, LANES), axis=-1, keepdims=True)
    le2 = jnp.where(lane == i1, NEG, le)
    m2 = jnp.max(le2, axis=-1, keepdims=True)
    i2 = jnp.min(jnp.where(le2 == m2, lane, LANES), axis=-1, keepdims=True)
    e2 = jnp.exp(m2 - m1)
    w1 = p_g / (1.0 + e2)
    w2 = p_g * e2 / (1.0 + e2)
    id_ref[...] = jnp.where(lane == 0, i1 - n_groups, jnp.where(lane == 1, i2 - n_groups, 0))
    wt_ref[...] = jnp.where(lane == 0, w1, jnp.where(lane == 1, w2, 0.0))


def _router(x, g, w, b, n_groups, per_group):
    t, d = x.shape
    tm = _tile(t, (256, 128, 64, 32, 16, 8))
    body = functools.partial(_router_body, n_groups=n_groups, per_group=per_group)
    return pl.pallas_call(
        body, grid=(t // tm,),
        in_specs=[pl.BlockSpec((tm, d), lambda i: (i, 0)), pl.BlockSpec((1, d), lambda i: (0, 0)),
                  pl.BlockSpec((d, LANES), lambda i: (0, 0)), pl.BlockSpec((1, LANES), lambda i: (0, 0))],
        out_specs=[pl.BlockSpec((tm, d), lambda i: (i, 0)), pl.BlockSpec((tm, LANES), lambda i: (i, 0)),
                   pl.BlockSpec((tm, LANES), lambda i: (i, 0))],
        out_shape=[jax.ShapeDtypeStruct((t, d), F32), jax.ShapeDtypeStruct((t, LANES), I32),
                   jax.ShapeDtypeStruct((t, LANES), F32)],
        compiler_params=_params("parallel"), name="ffn_norm_router")(x, g, w, b)


def _expert_body(src_ref, dst_ref, bexp_ref, cnt_ref, nvalid_ref, h_hbm, wup_ref, wdn_ref, wp_ref, y_hbm,
                 xbuf, obuf, sem_in, sem_out, *, rb, hidden):
    b = pl.program_id(0)
    nv = nvalid_ref[0]
    slot = b % 2

    def row_in(blk, sl, r):
        return pltpu.make_async_copy(h_hbm.at[pl.ds(src_ref[blk * rb + r], 1), :], xbuf.at[sl, pl.ds(r, 1), :],
                                     sem_in.at[sl])

    def row_out(blk, sl, r):
        return pltpu.make_async_copy(obuf.at[sl, pl.ds(r, 1), :], y_hbm.at[pl.ds(dst_ref[blk * rb + r], 1), :],
                                     sem_out.at[sl])

    def for_rows(blk, fn):
        def step(r, c):
            fn(r)
            return c
        lax.fori_loop(0, cnt_ref[blk], step, 0)

    @pl.when(b == 0)
    def _():
        xbuf[...] = jnp.zeros_like(xbuf)
        for_rows(0, lambda r: row_in(0, 0, r).start())

    @pl.when(b + 1 < nv)
    def _():
        for_rows(b + 1, lambda r: row_in(b + 1, 1 - slot, r).start())

    @pl.when(b < nv)
    def _():
        for_rows(b, lambda r: row_in(b, slot, r).wait())

        @pl.when(b >= 2)
        def _():
            for_rows(b - 2, lambda r: row_out(b - 2, slot, r).wait())

        x = xbuf[slot].astype(BF16)
        gu = jnp.dot(x, wup_ref[...], preferred_element_type=F32)
        gate, up = gu[:, :hidden], gu[:, hidden:]
        act = gate * (1.0 / (1.0 + jnp.exp(-gate))) * up
        out = jnp.dot(act.astype(BF16), wdn_ref[...], preferred_element_type=F32)
        obuf[slot] = out * wp_ref[...]
        for_rows(b, lambda r: row_out(b, slot, r).start())

        @pl.when(b == nv - 1)
        def _():
            @pl.when(b >= 1)
            def _():
                for_rows(b - 1, lambda r: row_out(b - 1, 1 - slot, r).wait())
            for_rows(b, lambda r: row_out(b, slot, r).wait())


def _experts(src, dst, bexp, cnt, nvalid, h, wup, wdn, wp, n_out_rows, rb):
    p = src.shape[0]
    n_blocks = p // rb
    d = h.shape[1]
    hidden = wdn.shape[1]
    grid_spec = pltpu.PrefetchScalarGridSpec(
        num_scalar_prefetch=5, grid=(n_blocks,),
        in_specs=[pl.BlockSpec(memory_space=pl.ANY),
                  pl.BlockSpec((None, d, 2 * hidden), lambda i, s, t, e, c, n: (e[i], 0, 0)),
                  pl.BlockSpec((None, hidden, d), lambda i, s, t, e, c, n: (e[i], 0, 0)),
                  pl.BlockSpec((rb, 1), lambda i, s, t, e, c, n: (i, 0))],
        out_specs=pl.BlockSpec(memory_space=pl.ANY),
        scratch_shapes=[pltpu.VMEM((2, rb, d), F32), pltpu.VMEM((2, rb, d), F32),
                        pltpu.SemaphoreType.DMA((2,)), pltpu.SemaphoreType.DMA((2,))])
    return pl.pallas_call(
        functools.partial(_expert_body, rb=rb, hidden=hidden), grid_spec=grid_spec,
        out_shape=jax.ShapeDtypeStruct((n_out_rows, d), F32),
        compiler_params=_params("arbitrary"), name="moe_experts")(src, dst, bexp, cnt, nvalid, h, wup, wdn, wp)


def _final_body(x_ref, y0_ref, y1_ref, g_ref, o_ref, *, norm):
    x = x_ref[...] + y0_ref[...] + y1_ref[...]
    o_ref[...] = _rms(x, g_ref[...]) if norm else x


def _final(x, y2, g, norm, row0, rows):
    d = x.shape[1]
    tm = _tile(math.gcd(row0, rows), (256, 128, 64, 32, 16, 8))
    i0 = row0 // tm
    return pl.pallas_call(
        functools.partial(_final_body, norm=norm), grid=(rows // tm,),
        in_specs=[pl.BlockSpec((tm, d), lambda i: (i0 + i, 0)), pl.BlockSpec((None, tm, d), lambda i: (0, i0 + i, 0)),
                  pl.BlockSpec((None, tm, d), lambda i: (1, i0 + i, 0)), pl.BlockSpec((1, d), lambda i: (0, 0))],
        out_specs=pl.BlockSpec((tm, d), lambda i: (i, 0)),
        out_shape=jax.ShapeDtypeStruct((rows, d), F32),
        compiler_params=_params("parallel"), name="final_norm")(x, y2, y2, g)


def _dispatch(ids, wts, t, n_experts, rb):
    eid = ids.reshape(-1)
    a = eid.shape[0]
    order = jnp.argsort(eid, stable=True)
    eid_s = eid[order]
    counts = jnp.bincount(eid, length=n_experts)
    padded = (counts + rb - 1) // rb * rb
    pend = jnp.cumsum(padded)
    pstart = pend - padded
    cstart = jnp.cumsum(counts) - counts
    dest = pstart[eid_s] + (jnp.arange(a) - cstart[eid_s])
    n_blocks = (a + rb - 1) // rb + n_experts
    p = n_blocks * rb
    tok = (order // MOE_TOPK).astype(I32)
    slot = (order % MOE_TOPK).astype(I32)
    src = jnp.full((p,), t, I32).at[dest].set(tok)
    slot_p = jnp.zeros((p,), I32).at[dest].set(slot)
    wp = jnp.zeros((p,), F32).at[dest].set(wts.reshape(-1)[order])
    bexp = jnp.minimum(jnp.searchsorted(pend, jnp.arange(n_blocks) * rb, side='right'), n_experts - 1).astype(I32)
    nvalid = (pend[-1] // rb).astype(I32).reshape(1)
    cnt = jnp.sum((src != t).reshape(n_blocks, rb), axis=1).astype(I32)
    return src, slot_p, wp, bexp, cnt, nvalid


def kernel(x_prompt, x_sample, cache_mla, cache_kv, page_table, g_attn_norm, w_in, g_q_norm, g_kv_norm, w_uq, w_uk,
           w_uv, w_branch_a, w_branch_b, w_out, g_ffn_norm, w_router_group, b_router_group, w_router_expert,
           b_router_expert, w_exp_up, w_exp_down, g_final):
    n_p, seq, d = x_prompt.shape
    n_s, n_new, _ = x_sample.shape
    depth = w_in.shape[0]
    n_pages, page = page_table.shape[1], cache_mla.shape[2]
    past = n_pages * page
    q_lora, heads_a = w_uq.shape[1], w_uq.shape[2]
    nope = w_uk.shape[3]
    kv_lora = w_uk.shape[1]
    lat_w = cache_mla.shape[3]
    kv_heads, hd = cache_kv.shape[4], cache_kv.shape[5]
    heads_b = w_branch_b.shape[1] // hd
    group = heads_b // kv_heads
    n_groups, per_group = b_router_expert.shape[1], b_router_expert.shape[2]
    n_experts = n_groups * per_group
    assert (kv_lora, nope, lat_w - kv_lora, hd, w_uv.shape[3]) == (512, 128, ROPE_DIM, 128, 128)
    assert n_groups + n_experts <= LANES
    tp, ts = n_p * seq, n_s * n_new
    t = tp + ts
    mla_scale = (nope + ROPE_DIM) ** -0.5
    moba_scale = hd ** -0.5
    kvw = kv_heads * hd

    half = ROPE_DIM // 2
    inv = ROPE_THETA ** (-jnp.arange(half, dtype=F32) / half)
    pos = jnp.concatenate([jnp.tile(jnp.arange(seq), n_p), jnp.tile(past + jnp.arange(n_new), n_s)])
    ang = pos.astype(F32)[:, None] * inv[None, :]
    zpad = jnp.zeros((t, ROPE_PAD - ROPE_DIM), F32)
    cos_t = jnp.concatenate([jnp.cos(ang), jnp.cos(ang), zpad], axis=1)
    sin_t = jnp.concatenate([-jnp.sin(ang), jnp.sin(ang), zpad], axis=1)

    x = jnp.concatenate([x_prompt.reshape(tp, d), x_sample.reshape(ts, d)], axis=0)
    new_mla_p, new_mla_s, new_kv_p, new_kv_s = [], [], [], []
    tm = _tile(t)
    for l in range(depth):
        o0 = q_lora
        o1 = o0 + lat_w
        o2 = o1 + heads_b * hd
        o3 = o2 + 2 * kvw
        wl = w_in[l]
        w_cq = wl[:, :o0].astype(BF16)
        w_ckr = jnp.pad(wl[:, o0:o1], ((0, 0), (0, 640 - lat_w))).astype(BF16)
        w_qb = wl[:, o1:o2].astype(BF16)
        w_kv = wl[:, o2:o3].astype(BF16)
        w_g = wl[:, o3:].astype(BF16)

        h = _rmsnorm(x, g_attn_norm[l], BF16)
        row_spec = lambda w: pl.BlockSpec((tm, w), lambda i, j: (i, 0))
        one_spec = lambda w: pl.BlockSpec((1, w), lambda i, j: (0, 0))
        (cq,) = _mm(h, w_cq, lambda acc, g: (_rms(acc, g),), [g_q_norm[l].reshape(1, -1)], [one_spec(q_lora)],
                    [jax.ShapeDtypeStruct((t, q_lora), BF16)], [pl.BlockSpec((tm, q_lora), lambda i, j: (i, 0))],
                    tm, q_lora, "in_proj_cq")
        mla_rows, kp = _mm(h, w_ckr, _ckr_fn, [g_kv_norm[l].reshape(1, -1), cos_t, sin_t],
                           [one_spec(kv_lora), row_spec(ROPE_PAD), row_spec(ROPE_PAD)],
                           [jax.ShapeDtypeStruct((t, lat_w), F32), jax.ShapeDtypeStruct((t, 640), BF16)],
                           [pl.BlockSpec((tm, lat_w), lambda i, j: (i, 0)), pl.BlockSpec((tm, 640), lambda i, j: (i, 0))],
                           tm, 640, "in_proj_ckr")
        tn_q = _tile(heads_b * hd, (1024, 512, 256, 128))
        (qb,) = _mm(h, w_qb, lambda acc: (acc,), [], [], [jax.ShapeDtypeStruct((t, heads_b * hd), BF16)],
                    [pl.BlockSpec((tm, tn_q), lambda i, j: (i, j))], tm, tn_q, "in_proj_qb")
        kv_rows, kvb = _mm(h, w_kv, lambda acc: (acc, acc), [], [],
                           [jax.ShapeDtypeStruct((t, 2 * kvw), F32), jax.ShapeDtypeStruct((t, 2 * kvw), BF16)],
                           [pl.BlockSpec((tm, 2 * kvw), lambda i, j: (i, 0))] * 2, tm, 2 * kvw, "in_proj_kv")
        tn_g = _tile(2 * d, (1024, 512, 256, 128))
        (gates,) = _mm(h, w_g, lambda acc: (1.0 / (1.0 + jnp.exp(-acc)),), [], [],
                       [jax.ShapeDtypeStruct((t, 2 * d), BF16)], [pl.BlockSpec((tm, tn_g), lambda i, j: (i, j))],
                       tm, tn_g, "in_proj_gates")

        wq = w_uq[l]
        wn = wq[:, :, :nope].reshape(q_lora, heads_a * nope).astype(BF16)
        wr = jnp.pad(wq[:, :, nope:], ((0, 0), (0, 0), (0, ROPE_PAD - ROPE_DIM)))
        wr = wr.reshape(q_lora, heads_a * ROPE_PAD).astype(BF16)
        wuk = w_uk[l].transpose(1, 2, 0).astype(BF16)
        wuv = w_uv[l].transpose(1, 0, 2).astype(BF16)
        qa = _q_proj(cq, wn, wr, wuk, cos_t, sin_t)

        oa_p = _mla_prompt(qa, kp, wuv, n_p, seq, mla_scale)
        km = _kmean_prompt(kv_rows, n_p, seq, kvw)
        ob_p = _moba_prompt(qb, kvb, km, n_p, seq, kv_heads, moba_scale)

        rows_a = heads_a * n_new
        qa_s = qa[:, tp:, :].reshape(heads_a, n_s, n_new, 640).transpose(1, 0, 2, 3).reshape(n_s, rows_a, 640)
        knew = jnp.pad(kp[tp:].reshape(n_s, n_new, 640), ((0, 0), (0, LANES - n_new), (0, 0)))
        lat_s = _mla_sample(page_table, qa_s, knew, jnp.swapaxes(cache_mla, 2, 3), l, n_new, mla_scale)
        lat_s = lat_s.reshape(n_s, heads_a, n_new, kv_lora).transpose(1, 0, 2, 3).reshape(heads_a, ts, kv_lora)
        oa_s = _uv_proj(lat_s, wuv)

        qb_s = qb[tp:].reshape(n_s, n_new, kv_heads, group, hd).transpose(0, 2, 3, 1, 4)
        qb_s = qb_s.reshape(n_s, kv_heads, group * n_new, hd)
        new_b = kvb[tp:].reshape(n_s, n_new, 2 * kv_heads, hd).transpose(0, 2, 1, 3)
        new_b = jnp.pad(new_b, ((0, 0), (0, 0), (0, LANES - n_new), (0, 0)))
        ckv = cache_kv.reshape(depth, cache_kv.shape[1], page * 2 * kv_heads, hd)
        ob_s = _moba_sample(page_table, qb_s, new_b, ckv, l, n_new, past, moba_scale)
        ob_s = ob_s.reshape(n_s, kv_heads, group, n_new, hd).transpose(0, 3, 1, 2, 4).reshape(ts, heads_b * hd)

        oa = jnp.concatenate([oa_p, oa_s], axis=0)
        ob = jnp.concatenate([ob_p, ob_s], axis=0)
        merged = _merge(oa, ob, w_branch_a[l].astype(BF16), w_branch_b[l].astype(BF16), gates)
        tn_o = _tile(d, (1024, 512, 256, 128))
        (x,) = _mm(merged, w_out[l].astype(BF16), lambda acc, xr: (acc + xr,), [x],
                   [pl.BlockSpec((tm, tn_o), lambda i, j: (i, j))], [jax.ShapeDtypeStruct((t, d), F32)],
                   [pl.BlockSpec((tm, tn_o), lambda i, j: (i, j))], tm, tn_o, "out_proj")

        new_mla_p.append(mla_rows[:tp].reshape(n_p, seq, lat_w))
        new_mla_s.append(mla_rows[tp:].reshape(n_s, n_new, lat_w))
        new_kv_p.append(kv_rows[:tp].reshape(n_p, seq, 2, kv_heads, hd))
        new_kv_s.append(kv_rows[tp:].reshape(n_s, n_new, 2, kv_heads, hd))

        w_r = jnp.concatenate([w_router_group[l], w_router_expert[l],
                               jnp.zeros((d, LANES - n_groups - n_experts), F32)], axis=1)
        b_r = jnp.concatenate([b_router_group[l], b_router_expert[l].reshape(-1),
                               jnp.zeros((LANES - n_groups - n_experts,), F32)]).reshape(1, LANES)
        h2, ids, wts = _router(x, g_ffn_norm[l].reshape(1, d), w_r, b_r, n_groups, per_group)
        rb = MOE_ROW_BLOCK
        src, slot, wp, bexp, cnt, nvalid = _dispatch(ids[:, :MOE_TOPK], wts[:, :MOE_TOPK], t, n_experts, rb)
        is_pad = src == t
        gsrc = jnp.where(is_pad, 0, src)
        dst = jnp.where(is_pad, 0, slot * t + src)
        y2 = _experts(gsrc, dst, bexp, cnt, nvalid, h2, w_exp_up[l].astype(BF16), w_exp_down[l].astype(BF16),
                      wp.reshape(-1, 1), MOE_TOPK * t, rb)
        y2 = y2.reshape(MOE_TOPK, t, d)
        if l + 1 < depth:
            x = _final(x, y2, jnp.ones((1, d), F32), False, 0, t)
    y_p = _final(x, y2, g_final.reshape(1, d), True, 0, tp)
    y_s = _final(x, y2, g_final.reshape(1, d), True, tp, ts)
    return (y_p.reshape(n_p, seq, d), y_s.reshape(n_s, n_new, d),
            jnp.stack(new_mla_p), jnp.stack(new_mla_s), jnp.stack(new_kv_p), jnp.stack(new_kv_s))
```

```python
import functools
import math

import jax
import jax.numpy as jnp
from jax import lax
from jax.experimental import pallas as pl
from jax.experimental.pallas import tpu as pltpu

F32 = jnp.float32
BF16 = jnp.bfloat16
I32 = jnp.int32

NORM_EPS = 1e-6
NEG = -1e30
ROPE_THETA = 10000.0
ROPE_DIM = 64
ROPE_PAD = 128
MOBA_BLOCK = 256
MOBA_TOPK = 3
MOE_TOPK = 2
MOE_ROW_BLOCK = 128
PAGE_SLOTS = 3
LOG2E = 1.4426950408889634
LANES = 128
VMEM_LIMIT = 56 * 1024 * 1024
NT_DIMS = (((1,), (1,)), ((), ()))


def _params(*sem):
    return pltpu.CompilerParams(dimension_semantics=sem, vmem_limit_bytes=VMEM_LIMIT)


def _tile(m, cands=(512, 256, 128, 64, 32, 16, 8)):
    for c in cands:
        if m % c == 0:
            return c
    raise ValueError(f"no tile for {m}")


def _rms(x, g):
    return x * lax.rsqrt(jnp.mean(x * x, axis=-1, keepdims=True) + NORM_EPS) * g


def _rope128(v, cos, sin):
    rot = pltpu.roll(v, 96, 1) + pltpu.roll(v, 32, 1)
    return v * cos + rot * sin


def _rmsnorm_body(x_ref, g_ref, o_ref):
    o_ref[...] = _rms(x_ref[...].astype(F32), g_ref[...]).astype(o_ref.dtype)


def _rmsnorm(x, g, out_dtype):
    m, d = x.shape
    tm = _tile(m)
    return pl.pallas_call(
        _rmsnorm_body, grid=(m // tm,),
        in_specs=[pl.BlockSpec((tm, d), lambda i: (i, 0)), pl.BlockSpec((1, d), lambda i: (0, 0))],
        out_specs=pl.BlockSpec((tm, d), lambda i: (i, 0)),
        out_shape=jax.ShapeDtypeStruct((m, d), out_dtype),
        compiler_params=_params("parallel"), name="rmsnorm")(x, g.reshape(1, d).astype(F32))


def _mm(a, b, fn, extras, extra_specs, out_shapes, out_specs, tm, tn, name):
    m, k = a.shape
    n = b.shape[1]
    n_ex = len(extras)

    def body(a_ref, b_ref, *refs):
        acc = jnp.dot(a_ref[...], b_ref[...], preferred_element_type=F32)
        res = fn(acc, *[r[...] for r in refs[:n_ex]])
        for o_ref, r in zip(refs[n_ex:], res):
            o_ref[...] = r.astype(o_ref.dtype)

    return pl.pallas_call(
        body, grid=(m // tm, n // tn),
        in_specs=[pl.BlockSpec((tm, k), lambda i, j: (i, 0)), pl.BlockSpec((k, tn), lambda i, j: (0, j))]
        + list(extra_specs),
        out_specs=out_specs, out_shape=out_shapes,
        compiler_params=_params("parallel", "arbitrary"), name=name)(a, b, *extras)


def _ckr_fn(acc, g, cos, sin):
    ckv = _rms(acc[:, :512], g)
    kr = _rope128(acc[:, 512:640], cos, sin)
    rows = jnp.concatenate([ckv, kr], axis=1)
    return rows[:, :576], rows


def _q_body(cq_ref, wn_ref, wr_ref, wuk_ref, cos_ref, sin_ref, q_ref, *, hb):
    cq = cq_ref[...]
    qn = jnp.dot(cq, wn_ref[...], preferred_element_type=F32).astype(BF16)
    qr = jnp.dot(cq, wr_ref[...], preferred_element_type=F32)
    cos, sin = cos_ref[...], sin_ref[...]
    for h in range(hb):
        lanes = slice(h * 128, (h + 1) * 128)
        q_ref[h, :, :512] = jnp.dot(qn[:, lanes], wuk_ref[h], preferred_element_type=F32).astype(BF16)
        q_ref[h, :, 512:] = _rope128(qr[:, lanes], cos, sin).astype(BF16)


def _q_proj(cq, wn, wr, wuk, cos, sin):
    t, c = cq.shape
    h = wuk.shape[0]
    hb = _tile(h, (4, 2, 1))
    tm = _tile(t)
    return pl.pallas_call(
        functools.partial(_q_body, hb=hb), grid=(t // tm, h // hb),
        in_specs=[pl.BlockSpec((tm, c), lambda i, j: (i, 0)),
                  pl.BlockSpec((c, hb * 128), lambda i, j: (0, j)),
                  pl.BlockSpec((c, hb * ROPE_PAD), lambda i, j: (0, j)),
                  pl.BlockSpec((hb, 128, 512), lambda i, j: (j, 0, 0)),
                  pl.BlockSpec((tm, ROPE_PAD), lambda i, j: (i, 0)),
                  pl.BlockSpec((tm, ROPE_PAD), lambda i, j: (i, 0))],
        out_specs=pl.BlockSpec((hb, tm, 640), lambda i, j: (j, i, 0)),
        out_shape=jax.ShapeDtypeStruct((h, t, 640), BF16),
        compiler_params=_params("parallel", "arbitrary"), name="q_proj")(cq, wn, wr, wuk, cos, sin)


def _softmax_step(s, v, m_scr, l_scr, acc_scr):
    m_prev = m_scr[...]
    m_new = jnp.maximum(m_prev, jnp.max(s, axis=-1, keepdims=True))
    alpha = jnp.exp2(m_prev - m_new)
    p = jnp.exp2(s - m_new)
    l_scr[...] = alpha * l_scr[...] + jnp.sum(p, axis=-1, keepdims=True)
    acc_scr[...] = alpha * acc_scr[...] + jnp.dot(p.astype(BF16), v, preferred_element_type=F32)
    m_scr[...] = m_new


def _rep(x, n):
    return x if n == LANES else jnp.concatenate([x] * (n // LANES), axis=1)


def _softmax_step_rep(s, v, m_scr, l_scr, acc_scr):
    m_prev = m_scr[...]
    m_new = jnp.maximum(m_prev, jnp.max(s, axis=-1, keepdims=True))
    alpha = jnp.exp2(m_prev - m_new)
    p = jnp.exp2(s - _rep(m_new, s.shape[1]))
    l_scr[...] = alpha * l_scr[...] + jnp.sum(p, axis=-1, keepdims=True)
    acc_scr[...] = _rep(alpha, acc_scr.shape[1]) * acc_scr[...] + jnp.dot(p.astype(BF16), v,
                                                                         preferred_element_type=F32)
    m_scr[...] = m_new


def _mla_prompt_body(q_ref, k_ref, wuv_ref, o_ref, m_scr, l_scr, acc_scr, *, tq, tk, heads, scale):
    i = pl.program_id(1)
    j = pl.program_id(2)
    last = (i * tq + tq - 1) // tk

    @pl.when(j == 0)
    def _():
        m_scr[...] = jnp.full_like(m_scr, NEG)
        l_scr[...] = jnp.zeros_like(l_scr)
        acc_scr[...] = jnp.zeros_like(acc_scr)

    def scores():
        q = q_ref[...].reshape(heads * tq, 640)
        k = k_ref[...]
        return lax.dot_general(q, k, NT_DIMS, preferred_element_type=F32) * scale, k[:, :512]

    @pl.when(j < last)
    def _():
        s, v = scores()
        _softmax_step_rep(s, v, m_scr, l_scr, acc_scr)

    @pl.when(j == last)
    def _():
        s, v = scores()
        row = lax.broadcasted_iota(I32, s.shape, 0)
        col = lax.broadcasted_iota(I32, s.shape, 1)
        qpos = i * tq + row % tq
        s = jnp.where(j * tk + col <= qpos, s, NEG)
        _softmax_step_rep(s, v, m_scr, l_scr, acc_scr)

    @pl.when(j == pl.num_programs(2) - 1)
    def _():
        lat = (acc_scr[...] / _rep(l_scr[...], acc_scr.shape[1])).astype(BF16)
        for h in range(heads):
            o_ref[:, h * 128:(h + 1) * 128] = jnp.dot(
                lat[h * tq:(h + 1) * tq, :], wuv_ref[h], preferred_element_type=F32).astype(o_ref.dtype)


def _mla_prompt(q, kp, wuv, n_seq, seq, scale):
    heads = q.shape[0]
    tq = _tile(seq, (128,))
    tk = _tile(seq, (512, 256, 128))
    assert tk % tq == 0
    nq, nk = seq // tq, seq // tk
    body = functools.partial(_mla_prompt_body, tq=tq, tk=tk, heads=heads, scale=scale * LOG2E)
    return pl.pallas_call(
        body, grid=(n_seq, nq, nk),
        in_specs=[pl.BlockSpec((heads, tq, 640), lambda n, i, j: (0, n * nq + i, 0)),
                  pl.BlockSpec((tk, 640), lambda n, i, j: (n * nk + jnp.minimum(j, (i * tq + tq - 1) // tk), 0)),
                  pl.BlockSpec((heads, 512, 128), lambda n, i, j: (0, 0, 0))],
        out_specs=pl.BlockSpec((tq, heads * 128), lambda n, i, j: (n * nq + i, 0)),
        out_shape=jax.ShapeDtypeStruct((n_seq * seq, heads * 128), BF16),
        scratch_shapes=[pltpu.VMEM((heads * tq, LANES), F32), pltpu.VMEM((heads * tq, LANES), F32),
                        pltpu.VMEM((heads * tq, 512), F32)],
        compiler_params=_params("parallel", "parallel", "arbitrary"), name="mla_prompt")(q, kp, wuv)


def _fetch_pages(pt_ref, cache_hbm, pbuf, sem, layer, n_par):
    n_slots = pbuf.shape[0]
    ahead = n_slots - 1
    ns = pl.num_programs(1)
    total = pl.num_programs(0) * ns
    n = pl.program_id(0) * ns + pl.program_id(1)

    def copies(step):
        bi, si, sl = step // ns, step % ns, step % n_slots
        return [pltpu.make_async_copy(cache_hbm.at[layer, pt_ref[bi, si * n_par + k]], pbuf.at[sl, k], sem.at[sl])
                for k in range(n_par)]

    @pl.when(n == 0)
    def _():
        for a in range(ahead):
            @pl.when(a < total)
            def _(a=a):
                for c in copies(a):
                    c.start()

    @pl.when(n + ahead < total)
    def _():
        for c in copies(n + ahead):
            c.start()

    for c in copies(n):
        c.wait()
    return n % n_slots


def _mla_sample_body(pt_ref, q_ref, knew_ref, cache_hbm, o_ref, pbuf, sem, kbuf, m_scr, l_scr, acc_scr, *,
                     layer, n_par, n_chunk, n_new, scale):
    s_idx = pl.program_id(1)
    width, page = pbuf.shape[2], pbuf.shape[3]
    slot = _fetch_pages(pt_ref, cache_hbm, pbuf, sem, layer, n_par)

    @pl.when(s_idx == 0)
    def _():
        m_scr[...] = jnp.full_like(m_scr, NEG)
        l_scr[...] = jnp.zeros_like(l_scr)
        acc_scr[...] = jnp.zeros_like(acc_scr)
        kbuf[width:, :] = jnp.zeros((kbuf.shape[0] - width, kbuf.shape[1]), BF16)

    for k in range(n_par):
        kbuf[:width, k * page:(k + 1) * page] = pbuf[slot, k].astype(BF16)
    q = q_ref[...]
    cw = n_par * page // n_chunk
    parts = []
    for c in range(n_chunk):
        kc = kbuf[:, c * cw:(c + 1) * cw]
        s = jnp.dot(q, kc, preferred_element_type=F32) * scale
        mc = jnp.max(s, axis=-1, keepdims=True)
        p = jnp.exp2(s - mc)
        oc = lax.dot_general(p.astype(BF16), kc[:512, :], NT_DIMS, preferred_element_type=F32)
        parts.append((mc, jnp.sum(p, axis=-1, keepdims=True), oc))
    m_prev = m_scr[...]
    m_new = m_prev
    for mc, _, _ in parts:
        m_new = jnp.maximum(m_new, mc)
    alpha = jnp.exp2(m_prev - m_new)
    l_new = alpha * l_scr[...]
    acc = alpha * acc_scr[...]
    for mc, lc, oc in parts:
        w = jnp.exp2(mc - m_new)
        l_new = l_new + w * lc
        acc = acc + w * oc
    m_scr[...] = m_new
    l_scr[...] = l_new
    acc_scr[...] = acc

    @pl.when(s_idx == pl.num_programs(1) - 1)
    def _():
        kn = knew_ref[...]
        sn = lax.dot_general(q, kn, NT_DIMS, preferred_element_type=F32) * scale
        row = lax.broadcasted_iota(I32, sn.shape, 0)
        col = lax.broadcasted_iota(I32, sn.shape, 1)
        sn = jnp.where(col <= row % n_new, sn, NEG)
        _softmax_step(sn, kn[:, :512], m_scr, l_scr, acc_scr)
        o_ref[...] = (acc_scr[...] / l_scr[...]).astype(o_ref.dtype)


def _mla_sample(page_table, q, knew, cache_t, layer, n_new, scale):
    b, rows, _ = q.shape
    n_pages = page_table.shape[1]
    width, page = cache_t.shape[2], cache_t.shape[3]
    n_par = _tile(n_pages, (32, 16, 8, 4, 2, 1))
    n_chunk = 1
    assert n_new <= LANES and width % 16 == 0 and width <= 640
    body = functools.partial(_mla_sample_body, layer=layer, n_par=n_par, n_chunk=n_chunk, n_new=n_new,
                             scale=scale * LOG2E)
    grid_spec = pltpu.PrefetchScalarGridSpec(
        num_scalar_prefetch=1, grid=(b, n_pages // n_par),
        in_specs=[pl.BlockSpec((None, rows, 640), lambda bi, si, pt: (bi, 0, 0)),
                  pl.BlockSpec((None, LANES, 640), lambda bi, si, pt: (bi, 0, 0)),
                  pl.BlockSpec(memory_space=pl.ANY)],
        out_specs=pl.BlockSpec((None, rows, 512), lambda bi, si, pt: (bi, 0, 0)),
        scratch_shapes=[pltpu.VMEM((PAGE_SLOTS, n_par, width, page), F32), pltpu.SemaphoreType.DMA((PAGE_SLOTS,)),
                        pltpu.VMEM((640, n_par * page), BF16), pltpu.VMEM((rows, 1), F32),
                        pltpu.VMEM((rows, 1), F32), pltpu.VMEM((rows, 512), F32)])
    return pl.pallas_call(
        body, grid_spec=grid_spec, out_shape=jax.ShapeDtypeStruct((b, rows, 512), BF16),
        compiler_params=_params("arbitrary", "arbitrary"), name="mla_sample")(page_table, q, knew, cache_t)


def _uv_body(lat_ref, w_ref, o_ref):
    o_ref[...] = jnp.dot(lat_ref[...], w_ref[...], preferred_element_type=F32).astype(o_ref.dtype)


def _uv_proj(lat, wuv):
    h, t, c = lat.shape
    return pl.pallas_call(
        _uv_body, grid=(h,),
        in_specs=[pl.BlockSpec((None, t, c), lambda j: (j, 0, 0)), pl.BlockSpec((None, c, 128), lambda j: (j, 0, 0))],
        out_specs=pl.BlockSpec((t, 128), lambda j: (0, j)),
        out_shape=jax.ShapeDtypeStruct((t, h * 128), BF16),
        compiler_params=_params("parallel"), name="uv_proj")(lat, wuv)


def _top_mask(gate, valid, topk):
    lane = lax.broadcasted_iota(I32, gate.shape, 1)
    g = jnp.where(valid, gate, NEG)
    sel = jnp.zeros(gate.shape, jnp.bool_)
    for _ in range(topk):
        mx = jnp.max(g, axis=-1, keepdims=True)
        idx = jnp.min(jnp.where(g == mx, lane, gate.shape[1]), axis=-1, keepdims=True)
        hit = lane == idx
        sel = sel | (hit & (mx > 0.5 * NEG))
        g = jnp.where(hit, -jnp.inf, g)
    return sel


def _kmean_body(k_ref, o_ref, *, nb):
    k = k_ref[...]
    w = k.shape[1]
    km = jnp.sum(k.reshape(nb, MOBA_BLOCK, w), axis=1) * (1.0 / MOBA_BLOCK)
    o_ref[...] = jnp.zeros_like(o_ref)
    o_ref[:nb, :] = km


def _kmean_prompt(kv, n_seq, seq, kw):
    nb = seq // MOBA_BLOCK
    assert nb <= LANES
    return pl.pallas_call(
        functools.partial(_kmean_body, nb=nb), grid=(n_seq,),
        in_specs=[pl.BlockSpec((seq, kw), lambda n: (n, 0))],
        out_specs=pl.BlockSpec((LANES, kw), lambda n: (n, 0)),
        out_shape=jax.ShapeDtypeStruct((n_seq * LANES, kw), F32),
        compiler_params=_params("parallel"), name="moba_kmean")(kv)


def _moba_prompt_body(q_ref, k_ref, v_ref, km_ref, o_ref, m_scr, l_scr, acc_scr, *, group, kv_heads, nb, scale):
    g = pl.program_id(1)
    i = pl.program_id(2)
    blk = MOBA_BLOCK
    rows = group * blk
    qv = q_ref[...]
    q = jnp.concatenate([qv[:, h * 128:(h + 1) * 128] for h in range(group)], axis=0)
    gate = lax.dot_general(q, km_ref[...].astype(BF16), NT_DIMS, preferred_element_type=F32)
    lane = lax.broadcasted_iota(I32, gate.shape, 1)
    sel = _top_mask(gate, lane < i, MOBA_TOPK)
    self = jnp.where(sel, 1.0, 0.0)

    row1 = lax.broadcasted_iota(I32, (rows, 1), 0)
    head = (g * group + row1 // blk + 1).astype(F32)
    slope = LOG2E * jnp.exp2(-8.0 * head / (group * kv_heads))
    row = lax.broadcasted_iota(I32, (rows, blk), 0)
    col = lax.broadcasted_iota(I32, (rows, blk), 1)
    rel = row % blk - col

    kd = k_ref[pl.ds(pl.multiple_of(i * blk, blk), blk), :]
    vd = v_ref[pl.ds(pl.multiple_of(i * blk, blk), blk), :]
    s = lax.dot_general(q, kd, NT_DIMS, preferred_element_type=F32) * scale - slope * rel.astype(F32)
    s = jnp.where(rel >= 0, s, NEG)
    m0 = jnp.max(s, axis=-1, keepdims=True)
    p = jnp.exp2(s - m0)
    m_scr[...] = jnp.broadcast_to(m0, m_scr.shape)
    l_scr[...] = jnp.broadcast_to(jnp.sum(p, axis=-1, keepdims=True), l_scr.shape)
    acc_scr[...] = jnp.dot(p.astype(BF16), vd, preferred_element_type=F32)

    for j in range(nb - 1):
        @pl.when(j < i)
        def _(j=j):
            kj = k_ref[j * blk:(j + 1) * blk, :]
            vj = v_ref[j * blk:(j + 1) * blk, :]
            dist = ((i - j) * blk + rel).astype(F32)
            sj = lax.dot_general(q, kj, NT_DIMS, preferred_element_type=F32) * scale - slope * dist
            sj = jnp.where(self[:, j:j + 1] > 0.5, sj, NEG)
            _softmax_step_rep(sj, vj, m_scr, l_scr, acc_scr)

    o = acc_scr[...] / l_scr[...]
    for h in range(group):
        o_ref[:, h * 128:(h + 1) * 128] = o[h * blk:(h + 1) * blk, :].astype(o_ref.dtype)


def _moba_prompt(q, kvb, km, n_seq, seq, kv_heads, scale):
    heads = q.shape[1] // 128
    group = heads // kv_heads
    nb = seq // MOBA_BLOCK
    rows = group * MOBA_BLOCK
    body = functools.partial(_moba_prompt_body, group=group, kv_heads=kv_heads, nb=nb, scale=scale * LOG2E)
    return pl.pallas_call(
        body, grid=(n_seq, kv_heads, nb),
        in_specs=[pl.BlockSpec((MOBA_BLOCK, group * 128), lambda n, g, i: (n * nb + i, g)),
                  pl.BlockSpec((seq, 128), lambda n, g, i: (n, g)),
                  pl.BlockSpec((seq, 128), lambda n, g, i: (n, kv_heads + g)),
                  pl.BlockSpec((LANES, 128), lambda n, g, i: (n, g))],
        out_specs=pl.BlockSpec((MOBA_BLOCK, group * 128), lambda n, g, i: (n * nb + i, g)),
        out_shape=jax.ShapeDtypeStruct((n_seq * seq, q.shape[1]), BF16),
        scratch_shapes=[pltpu.VMEM((rows, LANES), F32), pltpu.VMEM((rows, LANES), F32), pltpu.VMEM((rows, 128), F32)],
        compiler_params=_params("parallel", "parallel", "arbitrary"), name="moba_prompt")(q, kvb, kvb, km)


def _moba_sample_body(pt_ref, q_ref, new_ref, cache_hbm, o_ref, pbuf, sem, gate_scr, m_scr, l_scr, o_scr, *,
                      layer, n_par, page, kv_heads, group, n_new, past, scale):
    s_idx = pl.program_id(1)
    slot = _fetch_pages(pt_ref, cache_hbm, pbuf, sem, layer, n_par)
    ppb = MOBA_BLOCK // page
    bps = n_par // ppb
    nbp = past // MOBA_BLOCK
    rows = group * n_new
    lane = lax.broadcasted_iota(I32, (rows, LANES), 1)
    row1 = lax.broadcasted_iota(I32, (rows, 1), 0)
    t_q = row1 % n_new
    col = lax.broadcasted_iota(I32, (rows, MOBA_BLOCK), 1)

    def slope_of(g):
        head = (g * group + row1 // n_new + 1).astype(F32)
        return LOG2E * jnp.exp2(-8.0 * head / (group * kv_heads))

    @pl.when(s_idx == 0)
    def _():
        gate_scr[...] = jnp.full_like(gate_scr, NEG)
        m_scr[...] = jnp.full_like(m_scr, NEG)
        l_scr[...] = jnp.zeros_like(l_scr)

    n_which = 2 * kv_heads

    def rows_of(u, which):
        return pbuf[slot, u, pl.ds(which, page, stride=n_which), :].astype(BF16)

    for g in range(kv_heads):
        q = q_ref[g]
        slope = slope_of(g)
        bias = slope * col.astype(F32)
        k_all = jnp.concatenate([rows_of(u, g) for u in range(n_par)], axis=0)
        s_raw = lax.dot_general(q, k_all, NT_DIMS, preferred_element_type=F32)
        gate_new, m_new, l_new = gate_scr[g], m_scr[g], l_scr[g]
        for bb in range(bps):
            blk = s_idx * bps + bb
            sb_raw = s_raw[:, bb * MOBA_BLOCK:(bb + 1) * MOBA_BLOCK]
            gate = jnp.sum(sb_raw, axis=-1, keepdims=True) * (1.0 / MOBA_BLOCK)
            sb = sb_raw * scale + bias
            mb = jnp.max(sb, axis=-1, keepdims=True)
            p = jnp.exp2(sb - mb)
            v_b = jnp.concatenate([rows_of(bb * ppb + u, kv_heads + g) for u in range(ppb)], axis=0)
            o_scr[g, blk] = jnp.dot(p.astype(BF16), v_b, preferred_element_type=F32)
            row_c = slope * (past + t_q - blk * MOBA_BLOCK).astype(F32)
            hit = lane == blk
            gate_new = jnp.where(hit, gate, gate_new)
            m_new = jnp.where(hit, mb - row_c, m_new)
            l_new = jnp.where(hit, jnp.sum(p, axis=-1, keepdims=True), l_new)
        gate_scr[g] = gate_new
        m_scr[g] = m_new
        l_scr[g] = l_new

    @pl.when(s_idx == pl.num_programs(1) - 1)
    def _():
        for g in range(kv_heads):
            q = q_ref[g]
            kn = new_ref[g]
            vn = new_ref[kv_heads + g]
            coln = lax.broadcasted_iota(I32, (rows, LANES), 1)
            sn = lax.dot_general(q, kn, NT_DIMS, preferred_element_type=F32) * scale \
                - slope_of(g) * (t_q - coln).astype(F32)
            sn = jnp.where(coln <= t_q, sn, NEG)
            m_own = jnp.max(sn, axis=-1, keepdims=True)
            p_own = jnp.exp2(sn - m_own)
            l_own = jnp.sum(p_own, axis=-1, keepdims=True)
            o_own = jnp.dot(p_own.astype(BF16), vn, preferred_element_type=F32)

            sel = _top_mask(gate_scr[g], lane < nbp, MOBA_TOPK)
            mb = jnp.where(sel, m_scr[g], NEG)
            m_all = jnp.maximum(m_own, jnp.max(mb, axis=-1, keepdims=True))
            w = jnp.where(sel, jnp.exp2(mb - m_all), 0.0)
            w_own = jnp.exp2(m_own - m_all)
            l_all = w_own * l_own + jnp.sum(w * l_scr[g], axis=-1, keepdims=True)
            o_all = w_own * o_own
            for j in range(nbp):
                o_all = o_all + w[:, j:j + 1] * o_scr[g, j]
            o_ref[g] = (o_all / l_all).astype(o_ref.dtype)


def _moba_sample(page_table, q, new, cache, layer, n_new, past, scale):
    b, kv_heads, rows, _ = q.shape
    group = rows // n_new
    n_pages = page_table.shape[1]
    page_rows = cache.shape[2]
    page = page_rows // (2 * kv_heads)
    nbp = past // MOBA_BLOCK
    n_par = _tile(n_pages, (32, 16, 8, 4, 2))
    assert MOBA_BLOCK % page == 0 and n_par % (MOBA_BLOCK // page) == 0 and cache.shape[3] == 128
    assert past % MOBA_BLOCK == 0 and nbp <= LANES and n_new <= MOBA_BLOCK and n_new <= LANES
    body = functools.partial(_moba_sample_body, layer=layer, n_par=n_par, page=page, kv_heads=kv_heads, group=group,
                             n_new=n_new, past=past, scale=scale * LOG2E)
    grid_spec = pltpu.PrefetchScalarGridSpec(
        num_scalar_prefetch=1, grid=(b, n_pages // n_par),
        in_specs=[pl.BlockSpec((None, kv_heads, rows, 128), lambda bi, si, pt: (bi, 0, 0, 0)),
                  pl.BlockSpec((None, 2 * kv_heads, LANES, 128), lambda bi, si, pt: (bi, 0, 0, 0)),
                  pl.BlockSpec(memory_space=pl.ANY)],
        out_specs=pl.BlockSpec((None, kv_heads, rows, 128), lambda bi, si, pt: (bi, 0, 0, 0)),
        scratch_shapes=[pltpu.VMEM((PAGE_SLOTS, n_par, page_rows, 128), F32), pltpu.SemaphoreType.DMA((PAGE_SLOTS,)),
                        pltpu.VMEM((kv_heads, rows, LANES), F32), pltpu.VMEM((kv_heads, rows, LANES), F32),
                        pltpu.VMEM((kv_heads, rows, LANES), F32), pltpu.VMEM((kv_heads, nbp, rows, 128), F32)])
    return pl.pallas_call(
        body, grid_spec=grid_spec, out_shape=jax.ShapeDtypeStruct(q.shape, BF16),
        compiler_params=_params("arbitrary", "arbitrary"), name="moba_sample")(page_table, q, new, cache)


def _merge_body(oa_ref, ob_ref, wa_ref, wb_ref, ga_ref, gb_ref, o_ref):
    bra = jnp.dot(oa_ref[...], wa_ref[...], preferred_element_type=F32)
    brb = jnp.dot(ob_ref[...], wb_ref[...], preferred_element_type=F32)
    o_ref[...] = (ga_ref[...].astype(F32) * bra + gb_ref[...].astype(F32) * brb).astype(o_ref.dtype)


def _merge(oa, ob, wa, wb, gates):
    t, ka = oa.shape
    kb = ob.shape[1]
    d = wa.shape[1]
    tm, tn = _tile(t), _tile(d)
    nj = d // tn
    return pl.pallas_call(
        _merge_body, grid=(t // tm, nj),
        in_specs=[pl.BlockSpec((tm, ka), lambda i, j: (i, 0)), pl.BlockSpec((tm, kb), lambda i, j: (i, 0)),
                  pl.BlockSpec((ka, tn), lambda i, j: (0, j)), pl.BlockSpec((kb, tn), lambda i, j: (0, j)),
                  pl.BlockSpec((tm, tn), lambda i, j: (i, j)), pl.BlockSpec((tm, tn), lambda i, j: (i, nj + j))],
        out_specs=pl.BlockSpec((tm, tn), lambda i, j: (i, j)),
        out_shape=jax.ShapeDtypeStruct((t, d), BF16),
        compiler_params=_params("parallel", "arbitrary"), name="merge")(oa, ob, wa, wb, gates, gates)


def _router_body(x_ref, g_ref, w_ref, b_ref, h_ref, id_ref, wt_ref, *, n_groups, per_group):
    h = _rms(x_ref[...], g_ref[...])
    h_ref[...] = h
    logits = jnp.dot(h, w_ref[...], preferred_element_type=F32, precision=lax.Precision.HIGHEST) + b_ref[...]
    lane = lax.broadcasted_iota(I32, logits.shape, 1)
    lg = jnp.where(lane < n_groups, logits, NEG)
    mg = jnp.max(lg, axis=-1, keepdims=True)
    g_sel = jnp.min(jnp.where(lg == mg, lane, LANES), axis=-1, keepdims=True)
    p_g = 1.0 / jnp.sum(jnp.exp(lg - mg), axis=-1, keepdims=True)
    e_lane = lane - n_groups
    in_group = (e_lane >= g_sel * per_group) & (e_lane < (g_sel + 1) * per_group)
    le = jnp.where(in_group, logits, NEG)
    m1 = jnp.max(le, axis=-1, keepdims=True)
    i1 = jnp.min(jnp.where(le == m1, lane, LANES), axis=-1, keepdims=True)
    le2 = jnp.where(lane == i1, NEG, le)
    m2 = jnp.max(le2, axis=-1, keepdims=True)
    i2 = jnp.min(jnp.where(le2 == m2, lane, LANES), axis=-1, keepdims=True)
    e2 = jnp.exp(m2 - m1)
    w1 = p_g / (1.0 + e2)
    w2 = p_g * e2 / (1.0 + e2)
    id_ref[...] = jnp.where(lane == 0, i1 - n_groups, jnp.where(lane == 1, i2 - n_groups, 0))
    wt_ref[...] = jnp.where(lane == 0, w1, jnp.where(lane == 1, w2, 0.0))


def _router(x, g, w, b, n_groups, per_group):
    t, d = x.shape
    tm = _tile(t, (256, 128, 64, 32, 16, 8))
    body = functools.partial(_router_body, n_groups=n_groups, per_group=per_group)
    return pl.pallas_call(
        body, grid=(t // tm,),
        in_specs=[pl.BlockSpec((tm, d), lambda i: (i, 0)), pl.BlockSpec((1, d), lambda i: (0, 0)),
                  pl.BlockSpec((d, LANES), lambda i: (0, 0)), pl.BlockSpec((1, LANES), lambda i: (0, 0))],
        out_specs=[pl.BlockSpec((tm, d), lambda i: (i, 0)), pl.BlockSpec((tm, LANES), lambda i: (i, 0)),
                   pl.BlockSpec((tm, LANES), lambda i: (i, 0))],
        out_shape=[jax.ShapeDtypeStruct((t, d), F32), jax.ShapeDtypeStruct((t, LANES), I32),
                   jax.ShapeDtypeStruct((t, LANES), F32)],
        compiler_params=_params("parallel"), name="ffn_norm_router")(x, g, w, b)


def _expert_body(src_ref, dst_ref, bexp_ref, cnt_ref, nvalid_ref, h_hbm, wup_ref, wdn_ref, wp_ref, y_hbm,
                 xbuf, obuf, sem_in, sem_out, *, rb, hidden):
    b = pl.program_id(0)
    nv = nvalid_ref[0]
    slot = b % 2

    def row_in(blk, sl, r):
        return pltpu.make_async_copy(h_hbm.at[pl.ds(src_ref[blk * rb + r], 1), :], xbuf.at[sl, pl.ds(r, 1), :],
                                     sem_in.at[sl])

    def row_out(blk, sl, r):
        return pltpu.make_async_copy(obuf.at[sl, pl.ds(r, 1), :], y_hbm.at[pl.ds(dst_ref[blk * rb + r], 1), :],
                                     sem_out.at[sl])

    def for_rows(blk, fn):
        def step(r, c):
            fn(r)
            return c
        lax.fori_loop(0, cnt_ref[blk], step, 0)

    @pl.when(b == 0)
    def _():
        xbuf[...] = jnp.zeros_like(xbuf)
        for_rows(0, lambda r: row_in(0, 0, r).start())

    @pl.when(b + 1 < nv)
    def _():
        for_rows(b + 1, lambda r: row_in(b + 1, 1 - slot, r).start())

    @pl.when(b < nv)
    def _():
        for_rows(b, lambda r: row_in(b, slot, r).wait())

        @pl.when(b >= 2)
        def _():
            for_rows(b - 2, lambda r: row_out(b - 2, slot, r).wait())

        x = xbuf[slot].astype(BF16)
        gu = jnp.dot(x, wup_ref[...], preferred_element_type=F32)
        gate, up = gu[:, :hidden], gu[:, hidden:]
        act = gate * (1.0 / (1.0 + jnp.exp(-gate))) * up
        out = jnp.dot(act.astype(BF16), wdn_ref[...], preferred_element_type=F32)
        obuf[slot] = out * wp_ref[...]
        for_rows(b, lambda r: row_out(b, slot, r).start())

        @pl.when(b == nv - 1)
        def _():
            @pl.when(b >= 1)
            def _():
                for_rows(b - 1, lambda r: row_out(b - 1, 1 - slot, r).wait())
            for_rows(b, lambda r: row_out(b, slot, r).wait())


def _experts(src, dst, bexp, cnt, nvalid, h, wup, wdn, wp, n_out_rows, rb):
    p = src.shape[0]
    n_blocks = p // rb
    d = h.shape[1]
    hidden = wdn.shape[1]
    grid_spec = pltpu.PrefetchScalarGridSpec(
        num_scalar_prefetch=5, grid=(n_blocks,),
        in_specs=[pl.BlockSpec(memory_space=pl.ANY),
                  pl.BlockSpec((None, d, 2 * hidden), lambda i, s, t, e, c, n: (e[i], 0, 0)),
                  pl.BlockSpec((None, hidden, d), lambda i, s, t, e, c, n: (e[i], 0, 0)),
                  pl.BlockSpec((rb, 1), lambda i, s, t, e, c, n: (i, 0))],
        out_specs=pl.BlockSpec(memory_space=pl.ANY),
        scratch_shapes=[pltpu.VMEM((2, rb, d), F32), pltpu.VMEM((2, rb, d), F32),
                        pltpu.SemaphoreType.DMA((2,)), pltpu.SemaphoreType.DMA((2,))])
    return pl.pallas_call(
        functools.partial(_expert_body, rb=rb, hidden=hidden), grid_spec=grid_spec,
        out_shape=jax.ShapeDtypeStruct((n_out_rows, d), F32),
        compiler_params=_params("arbitrary"), name="moe_experts")(src, dst, bexp, cnt, nvalid, h, wup, wdn, wp)


def _final_body(x_ref, y0_ref, y1_ref, g_ref, o_ref, *, norm):
    x = x_ref[...] + y0_ref[...] + y1_ref[...]
    o_ref[...] = _rms(x, g_ref[...]) if norm else x


def _final(x, y2, g, norm, row0, rows):
    d = x.shape[1]
    tm = _tile(math.gcd(row0, rows), (256, 128, 64, 32, 16, 8))
    i0 = row0 // tm
    return pl.pallas_call(
        functools.partial(_final_body, norm=norm), grid=(rows // tm,),
        in_specs=[pl.BlockSpec((tm, d), lambda i: (i0 + i, 0)), pl.BlockSpec((None, tm, d), lambda i: (0, i0 + i, 0)),
                  pl.BlockSpec((None, tm, d), lambda i: (1, i0 + i, 0)), pl.BlockSpec((1, d), lambda i: (0, 0))],
        out_specs=pl.BlockSpec((tm, d), lambda i: (i, 0)),
        out_shape=jax.ShapeDtypeStruct((rows, d), F32),
        compiler_params=_params("parallel"), name="final_norm")(x, y2, y2, g)


def _dispatch(ids, wts, t, n_experts, rb):
    eid = ids.reshape(-1)
    a = eid.shape[0]
    order = jnp.argsort(eid, stable=True)
    eid_s = eid[order]
    counts = jnp.bincount(eid, length=n_experts)
    padded = (counts + rb - 1) // rb * rb
    pend = jnp.cumsum(padded)
    pstart = pend - padded
    cstart = jnp.cumsum(counts) - counts
    dest = pstart[eid_s] + (jnp.arange(a) - cstart[eid_s])
    n_blocks = (a + rb - 1) // rb + n_experts
    p = n_blocks * rb
    tok = (order // MOE_TOPK).astype(I32)
    slot = (order % MOE_TOPK).astype(I32)
    src = jnp.full((p,), t, I32).at[dest].set(tok)
    slot_p = jnp.zeros((p,), I32).at[dest].set(slot)
    wp = jnp.zeros((p,), F32).at[dest].set(wts.reshape(-1)[order])
    bexp = jnp.minimum(jnp.searchsorted(pend, jnp.arange(n_blocks) * rb, side='right'), n_experts - 1).astype(I32)
    nvalid = (pend[-1] // rb).astype(I32).reshape(1)
    cnt = jnp.sum((src != t).reshape(n_blocks, rb), axis=1).astype(I32)
    return src, slot_p, wp, bexp, cnt, nvalid


def kernel(x_prompt, x_sample, cache_mla, cache_kv, page_table, g_attn_norm, w_in, g_q_norm, g_kv_norm, w_uq, w_uk,
           w_uv, w_branch_a, w_branch_b, w_out, g_ffn_norm, w_router_group, b_router_group, w_router_expert,
           b_router_expert, w_exp_up, w_exp_down, g_final):
    n_p, seq, d = x_prompt.shape
    n_s, n_new, _ = x_sample.shape
    depth = w_in.shape[0]
    n_pages, page = page_table.shape[1], cache_mla.shape[2]
    past = n_pages * page
    q_lora, heads_a = w_uq.shape[1], w_uq.shape[2]
    nope = w_uk.shape[3]
    kv_lora = w_uk.shape[1]
    lat_w = cache_mla.shape[3]
    kv_heads, hd = cache_kv.shape[4], cache_kv.shape[5]
    heads_b = w_branch_b.shape[1] // hd
    group = heads_b // kv_heads
    n_groups, per_group = b_router_expert.shape[1], b_router_expert.shape[2]
    n_experts = n_groups * per_group
    assert (kv_lora, nope, lat_w - kv_lora, hd, w_uv.shape[3]) == (512, 128, ROPE_DIM, 128, 128)
    assert n_groups + n_experts <= LANES
    tp, ts = n_p * seq, n_s * n_new
    t = tp + ts
    mla_scale = (nope + ROPE_DIM) ** -0.5
    moba_scale = hd ** -0.5
    kvw = kv_heads * hd

    half = ROPE_DIM // 2
    inv = ROPE_THETA ** (-jnp.arange(half, dtype=F32) / half)
    pos = jnp.concatenate([jnp.tile(jnp.arange(seq), n_p), jnp.tile(past + jnp.arange(n_new), n_s)])
    ang = pos.astype(F32)[:, None] * inv[None, :]
    zpad = jnp.zeros((t, ROPE_PAD - ROPE_DIM), F32)
    cos_t = jnp.concatenate([jnp.cos(ang), jnp.cos(ang), zpad], axis=1)
    sin_t = jnp.concatenate([-jnp.sin(ang), jnp.sin(ang), zpad], axis=1)

    x = jnp.concatenate([x_prompt.reshape(tp, d), x_sample.reshape(ts, d)], axis=0)
    new_mla_p, new_mla_s, new_kv_p, new_kv_s = [], [], [], []
    tm = _tile(t)
    for l in range(depth):
        o0 = q_lora
        o1 = o0 + lat_w
        o2 = o1 + heads_b * hd
        o3 = o2 + 2 * kvw
        wl = w_in[l]
        w_cq = wl[:, :o0].astype(BF16)
        w_ckr = jnp.pad(wl[:, o0:o1], ((0, 0), (0, 640 - lat_w))).astype(BF16)
        w_qb = wl[:, o1:o2].astype(BF16)
        w_kv = wl[:, o2:o3].astype(BF16)
        w_g = wl[:, o3:].astype(BF16)

        h = _rmsnorm(x, g_attn_norm[l], BF16)
        row_spec = lambda w: pl.BlockSpec((tm, w), lambda i, j: (i, 0))
        one_spec = lambda w: pl.BlockSpec((1, w), lambda i, j: (0, 0))
        (cq,) = _mm(h, w_cq, lambda acc, g: (_rms(acc, g),), [g_q_norm[l].reshape(1, -1)], [one_spec(q_lora)],
                    [jax.ShapeDtypeStruct((t, q_lora), BF16)], [pl.BlockSpec((tm, q_lora), lambda i, j: (i, 0))],
                    tm, q_lora, "in_proj_cq")
        mla_rows, kp = _mm(h, w_ckr, _ckr_fn, [g_kv_norm[l].reshape(1, -1), cos_t, sin_t],
                           [one_spec(kv_lora), row_spec(ROPE_PAD), row_spec(ROPE_PAD)],
                           [jax.ShapeDtypeStruct((t, lat_w), F32), jax.ShapeDtypeStruct((t, 640), BF16)],
                           [pl.BlockSpec((tm, lat_w), lambda i, j: (i, 0)), pl.BlockSpec((tm, 640), lambda i, j: (i, 0))],
                           tm, 640, "in_proj_ckr")
        tn_q = _tile(heads_b * hd, (1024, 512, 256, 128))
        (qb,) = _mm(h, w_qb, lambda acc: (acc,), [], [], [jax.ShapeDtypeStruct((t, heads_b * hd), BF16)],
                    [pl.BlockSpec((tm, tn_q), lambda i, j: (i, j))], tm, tn_q, "in_proj_qb")
        kv_rows, kvb = _mm(h, w_kv, lambda acc: (acc, acc), [], [],
                           [jax.ShapeDtypeStruct((t, 2 * kvw), F32), jax.ShapeDtypeStruct((t, 2 * kvw), BF16)],
                           [pl.BlockSpec((tm, 2 * kvw), lambda i, j: (i, 0))] * 2, tm, 2 * kvw, "in_proj_kv")
        tn_g = _tile(2 * d, (1024, 512, 256, 128))
        (gates,) = _mm(h, w_g, lambda acc: (1.0 / (1.0 + jnp.exp(-acc)),), [], [],
                       [jax.ShapeDtypeStruct((t, 2 * d), BF16)], [pl.BlockSpec((tm, tn_g), lambda i, j: (i, j))],
                       tm, tn_g, "in_proj_gates")

        wq = w_uq[l]
        wn = wq[:, :, :nope].reshape(q_lora, heads_a * nope).astype(BF16)
        wr = jnp.pad(wq[:, :, nope:], ((0, 0), (0, 0), (0, ROPE_PAD - ROPE_DIM)))
        wr = wr.reshape(q_lora, heads_a * ROPE_PAD).astype(BF16)
        wuk = w_uk[l].transpose(1, 2, 0).astype(BF16)
        wuv = w_uv[l].transpose(1, 0, 2).astype(BF16)
        qa = _q_proj(cq, wn, wr, wuk, cos_t, sin_t)

        oa_p = _mla_prompt(qa, kp, wuv, n_p, seq, mla_scale)
        km = _kmean_prompt(kv_rows, n_p, seq, kvw)
        ob_p = _moba_prompt(qb, kvb, km, n_p, seq, kv_heads, moba_scale)

        rows_a = heads_a * n_new
        qa_s = qa[:, tp:, :].reshape(heads_a, n_s, n_new, 640).transpose(1, 0, 2, 3).reshape(n_s, rows_a, 640)
        knew = jnp.pad(kp[tp:].reshape(n_s, n_new, 640), ((0, 0), (0, LANES - n_new), (0, 0)))
        lat_s = _mla_sample(page_table, qa_s, knew, jnp.swapaxes(cache_mla, 2, 3), l, n_new, mla_scale)
        lat_s = lat_s.reshape(n_s, heads_a, n_new, kv_lora).transpose(1, 0, 2, 3).reshape(heads_a, ts, kv_lora)
        oa_s = _uv_proj(lat_s, wuv)

        qb_s = qb[tp:].reshape(n_s, n_new, kv_heads, group, hd).transpose(0, 2, 3, 1, 4)
        qb_s = qb_s.reshape(n_s, kv_heads, group * n_new, hd)
        new_b = kvb[tp:].reshape(n_s, n_new, 2 * kv_heads, hd).transpose(0, 2, 1, 3)
        new_b = jnp.pad(new_b, ((0, 0), (0, 0), (0, LANES - n_new), (0, 0)))
        ckv = cache_kv.reshape(depth, cache_kv.shape[1], page * 2 * kv_heads, hd)
        ob_s = _moba_sample(page_table, qb_s, new_b, ckv, l, n_new, past, moba_scale)
        ob_s = ob_s.reshape(n_s, kv_heads, group, n_new, hd).transpose(0, 3, 1, 2, 4).reshape(ts, heads_b * hd)

        oa = jnp.concatenate([oa_p, oa_s], axis=0)
        ob = jnp.concatenate([ob_p, ob_s], axis=0)
        merged = _merge(oa, ob, w_branch_a[l].astype(BF16), w_branch_b[l].astype(BF16), gates)
        tn_o = _tile(d, (1024, 512, 256, 128))
        (x,) = _mm(merged, w_out[l].astype(BF16), lambda acc, xr: (acc + xr,), [x],
                   [pl.BlockSpec((tm, tn_o), lambda i, j: (i, j))], [jax.ShapeDtypeStruct((t, d), F32)],
                   [pl.BlockSpec((tm, tn_o), lambda i, j: (i, j))], tm, tn_o, "out_proj")

        new_mla_p.append(mla_rows[:tp].reshape(n_p, seq, lat_w))
        new_mla_s.append(mla_rows[tp:].reshape(n_s, n_new, lat_w))
        new_kv_p.append(kv_rows[:tp].reshape(n_p, seq, 2, kv_heads, hd))
        new_kv_s.append(kv_rows[tp:].reshape(n_s, n_new, 2, kv_heads, hd))

        w_r = jnp.concatenate([w_router_group[l], w_router_expert[l],
                               jnp.zeros((d, LANES - n_groups - n_experts), F32)], axis=1)
        b_r = jnp.concatenate([b_router_group[l], b_router_expert[l].reshape(-1),
                               jnp.zeros((LANES - n_groups - n_experts,), F32)]).reshape(1, LANES)
        h2, ids, wts = _router(x, g_ffn_norm[l].reshape(1, d), w_r, b_r, n_groups, per_group)
        rb = MOE_ROW_BLOCK
        src, slot, wp, bexp, cnt, nvalid = _dispatch(ids[:, :MOE_TOPK], wts[:, :MOE_TOPK], t, n_experts, rb)
        is_pad = src == t
        gsrc = jnp.where(is_pad, 0, src)
        dst = jnp.where(is_pad, 0, slot * t + src)
        y2 = _experts(gsrc, dst, bexp, cnt, nvalid, h2, w_exp_up[l].astype(BF16), w_exp_down[l].astype(BF16),
                      wp.reshape(-1, 1), MOE_TOPK * t, rb)
        y2 = y2.reshape(MOE_TOPK, t, d)
        if l + 1 < depth:
            x = _final(x, y2, jnp.ones((1, d), F32), False, 0, t)
    y_p = _final(x, y2, g_final.reshape(1, d), True, 0, tp)
    y_s = _final(x, y2, g_final.reshape(1, d), True, tp, ts)
    return (y_p.reshape(n_p, seq, d), y_s.reshape(n_s, n_new, d),
            jnp.stack(new_mla_p), jnp.stack(new_mla_s), jnp.stack(new_kv_p), jnp.stack(new_kv_s))
```
